```python
import math
import jax, jax.numpy as jnp
from jax import lax
import numpy as np

D_MODEL = 2048
BATCH = 4
SEQ = 2048
DEPTH = 1
DEC_BATCH = 8
DEC_SEQ = 1
PAST_LEN = 16384
PAGE_SIZE = 128

ATT_WIDTH = D_MODEL // 2
CONV_WIDTH = D_MODEL - ATT_WIDTH
HEAD_DIM = 64
QK_DIM = 2 * HEAD_DIM
V_DIM = 2 * HEAD_DIM
N_HEADS = ATT_WIDTH // V_DIM
ROT_DIM = HEAD_DIM // 4
ROPE_THETA = 500000.0
CONV_KERNEL = 31
PLE_DIM = 256
Q_BLOCK = 128
EPS = 1e-6
D_IN = 4 * ATT_WIDTH + 3 * CONV_WIDTH
IN_SPLITS = (ATT_WIDTH, 2 * ATT_WIDTH, 3 * ATT_WIDTH, 4 * ATT_WIDTH, 4 * ATT_WIDTH + 2 * CONV_WIDTH)
NEG_INF = -1e30

kernel_name = "hymba_diffattn_conformer_decode_step"


def rms_norm(x, g):
    xf = x.astype(jnp.float32)
    y = xf * lax.rsqrt(jnp.mean(xf * xf, axis=-1, keepdims=True) + EPS)
    return (y * g.astype(jnp.float32)).astype(x.dtype)


def layer_norm(x, g, b):
    xf = x.astype(jnp.float32)
    mu = jnp.mean(xf, axis=-1, keepdims=True)
    var = jnp.mean(jnp.square(xf - mu), axis=-1, keepdims=True)
    y = (xf - mu) * lax.rsqrt(var + EPS)
    return (y * g.astype(jnp.float32) + b.astype(jnp.float32)).astype(x.dtype)


def partial_rope(x, pos):
    inv = ROPE_THETA ** (-jnp.arange(0, ROT_DIM, 2, dtype=jnp.float32) / ROT_DIM)
    ang = pos.astype(jnp.float32)[:, None] * inv[None, :]
    cos = jnp.cos(ang)[None, :, None, None, :]
    sin = jnp.sin(ang)[None, :, None, None, :]
    xr = x[..., :ROT_DIM].astype(jnp.float32)
    x1, x2 = xr[..., : ROT_DIM // 2], xr[..., ROT_DIM // 2:]
    rot = jnp.concatenate([x1 * cos - x2 * sin, x2 * cos + x1 * sin], axis=-1).astype(x.dtype)
    return jnp.concatenate([rot, x[..., ROT_DIM:]], axis=-1)


def diff_attend(q, k, v, q_pos, k_pos, lam):
    s = jnp.einsum("bqhmd,bkhmd->bhmqk", q, k).astype(jnp.float32) * (HEAD_DIM ** -0.5)
    mask = k_pos[None, :] <= q_pos[:, None]
    s = jnp.where(mask, s, NEG_INF)
    p = jax.nn.softmax(s, axis=-1)
    a = p[:, :, 0] - lam * p[:, :, 1]
    return jnp.einsum("bhqk,bkhe->bqhe", a.astype(v.dtype), v)


def prompt_attention(q, k, v, pos, lam):
    b, s = q.shape[0], q.shape[1]
    nqb = s // Q_BLOCK
    qb = q.reshape(b, nqb, Q_BLOCK, N_HEADS, 2, HEAD_DIM).transpose(1, 0, 2, 3, 4, 5)
    pb = pos.reshape(nqb, Q_BLOCK)
    out = lax.map(lambda xs: diff_attend(xs[0], k, v, xs[1], pos, lam), (qb, pb))
    return out.transpose(1, 0, 2, 3, 4).reshape(b, s, N_HEADS, V_DIM)


def causal_dwconv(hist, w_dw, b_dw):
    y = lax.conv_general_dilated(hist, w_dw[:, None, :].astype(hist.dtype), window_strides=(1,),
                                 padding="VALID", dimension_numbers=("NWC", "WIO", "NWC"),
                                 feature_group_count=CONV_WIDTH)
    return y + b_dw


def mixer_layer(h, p_l, pos, conv_hist, attn_fn, layer_idx, w_in, g_norm, lam_q1, lam_k1, lam_q2, lam_k2,
                g_subln, w_dw, b_dw, g_cln, b_cln, w_pw, w_out, g_ple, w_pg, w_ple):
    b, s, _ = h.shape
    u = rms_norm(h, g_norm)
    z = u @ w_in
    q, k, v, gate_a, cu, gate_c = jnp.split(z, IN_SPLITS, axis=-1)
    q = partial_rope(q.reshape(b, s, N_HEADS, 2, HEAD_DIM), pos)
    k = partial_rope(k.reshape(b, s, N_HEADS, 2, HEAD_DIM), pos)
    v = v.reshape(b, s, N_HEADS, V_DIM)
    lam_init = 0.8 - 0.6 * math.exp(-0.3 * layer_idx)
    lam = (jnp.exp(jnp.sum(lam_q1.astype(jnp.float32) * lam_k1.astype(jnp.float32)))
           - jnp.exp(jnp.sum(lam_q2.astype(jnp.float32) * lam_k2.astype(jnp.float32))) + lam_init)
    o = attn_fn(q, k, v, lam)
    o = rms_norm(o, g_subln) * (1.0 - lam_init)
    o_att = o.reshape(b, s, ATT_WIDTH) * jax.nn.silu(gate_a)
    ca, cb = jnp.split(cu, 2, axis=-1)
    g = ca * jax.nn.sigmoid(cb)
    hist = jnp.concatenate([conv_hist.astype(g.dtype), g], axis=1)
    c = causal_dwconv(hist, w_dw, b_dw)
    c = jax.nn.silu(layer_norm(c, g_cln, b_cln)) @ w_pw
    o_conv = c * jax.nn.silu(gate_c)
    h = h + jnp.concatenate([o_att, o_conv], axis=-1) @ w_out
    gate = jax.nn.sigmoid(rms_norm(h, g_ple) @ w_pg)
    h = h + gate * (p_l @ w_ple)
    new_conv = hist[:, -(CONV_KERNEL - 1):]
    return h, k.reshape(b, s, N_HEADS, QK_DIM), v, new_conv


def setup_inputs(seed: int = 0) -> dict:
    key = jax.random.key(seed)
    ks = jax.random.split(key, 32)
    n_pages = PAST_LEN // PAGE_SIZE
    n_used = DEC_BATCH * n_pages
    n_pool = n_used + n_used // 4
    f32 = jnp.float32
    nrm = lambda k, shp, sc: jax.random.normal(k, shp, f32) * sc
    perm = jax.random.permutation(ks[0], n_pool)[:n_used]
    return {
        "x_prompt": nrm(ks[1], (BATCH, SEQ, D_MODEL), 1.0),
        "x_sample": nrm(ks[2], (DEC_BATCH, DEC_SEQ, D_MODEL), 1.0),
        "p_prompt": nrm(ks[3], (DEPTH, BATCH, SEQ, PLE_DIM), 1.0),
        "p_sample": nrm(ks[4], (DEPTH, DEC_BATCH, DEC_SEQ, PLE_DIM), 1.0),
        "cache_k": nrm(ks[5], (DEPTH, n_pool, PAGE_SIZE, N_HEADS, QK_DIM), 1.0),
        "cache_v": nrm(ks[6], (DEPTH, n_pool, PAGE_SIZE, N_HEADS, V_DIM), 1.0),
        "state_conv": nrm(ks[7], (DEPTH, DEC_BATCH, CONV_KERNEL - 1, CONV_WIDTH), 1.0),
        "page_table": perm.reshape(DEC_BATCH, n_pages).astype(jnp.int32),
        "w_in": nrm(ks[8], (DEPTH, D_MODEL, D_IN), D_MODEL ** -0.5),
        "g_norm": 1.0 + nrm(ks[9], (DEPTH, D_MODEL), 0.02),
        "lam_q1": nrm(ks[10], (DEPTH, HEAD_DIM), 0.1),
        "lam_k1": nrm(ks[11], (DEPTH, HEAD_DIM), 0.1),
        "lam_q2": nrm(ks[12], (DEPTH, HEAD_DIM), 0.1),
        "lam_k2": nrm(ks[13], (DEPTH, HEAD_DIM), 0.1),
        "g_subln": 1.0 + nrm(ks[14], (DEPTH, V_DIM), 0.02),
        "w_dw": nrm(ks[15], (DEPTH, CONV_KERNEL, CONV_WIDTH), CONV_KERNEL ** -0.5),
        "b_dw": nrm(ks[16], (DEPTH, CONV_WIDTH), 0.02),
        "g_cln": 1.0 + nrm(ks[17], (DEPTH, CONV_WIDTH), 0.02),
        "b_cln": nrm(ks[18], (DEPTH, CONV_WIDTH), 0.02),
        "w_pw": nrm(ks[19], (DEPTH, CONV_WIDTH, CONV_WIDTH), CONV_WIDTH ** -0.5),
        "w_out": nrm(ks[20], (DEPTH, ATT_WIDTH + CONV_WIDTH, D_MODEL), (ATT_WIDTH + CONV_WIDTH) ** -0.5),
        "g_ple": 1.0 + nrm(ks[21], (DEPTH, D_MODEL), 0.02),
        "w_pg": nrm(ks[22], (DEPTH, D_MODEL, D_MODEL), D_MODEL ** -0.5),
        "w_ple": nrm(ks[23], (DEPTH, PLE_DIM, D_MODEL), PLE_DIM ** -0.5),
        "g_final": 1.0 + nrm(ks[24], (D_MODEL,), 0.02),
    }


def reference(x_prompt, x_sample, p_prompt, p_sample, cache_k, cache_v, state_conv, page_table,
              w_in, g_norm, lam_q1, lam_k1, lam_q2, lam_k2, g_subln, w_dw, b_dw, g_cln, b_cln,
              w_pw, w_out, g_ple, w_pg, w_ple, g_final):
    n_pages = PAST_LEN // PAGE_SIZE
    past = n_pages * PAGE_SIZE
    pos_p = jnp.arange(SEQ, dtype=jnp.int32)
    pos_s = PAST_LEN + jnp.arange(DEC_SEQ, dtype=jnp.int32)
    k_pos_s = jnp.arange(past + DEC_SEQ, dtype=jnp.int32)
    h_p, h_s = x_prompt, x_sample
    nk_p, nv_p, nc_p, nk_s, nv_s, nc_s = [], [], [], [], [], []
    for i in range(DEPTH):
        params = (w_in[i], g_norm[i], lam_q1[i], lam_k1[i], lam_q2[i], lam_k2[i], g_subln[i],
                  w_dw[i], b_dw[i], g_cln[i], b_cln[i], w_pw[i], w_out[i], g_ple[i], w_pg[i], w_ple[i])

        def prompt_attn(q, k, v, lam):
            return prompt_attention(q, k, v, pos_p, lam)

        ck, cv = cache_k[i], cache_v[i]

        def sample_attn(q, k, v, lam, ck=ck, cv=cv):
            pk = ck[page_table].reshape(DEC_BATCH, past, N_HEADS, 2, HEAD_DIM).astype(k.dtype)
            pv = cv[page_table].reshape(DEC_BATCH, past, N_HEADS, V_DIM).astype(v.dtype)
            k_all = jnp.concatenate([pk, k], axis=1)
            v_all = jnp.concatenate([pv, v], axis=1)
            return diff_attend(q, k_all, v_all, pos_s, k_pos_s, lam)

        zero_hist = jnp.zeros((BATCH, CONV_KERNEL - 1, CONV_WIDTH), x_prompt.dtype)
        h_p, kp, vp, cp = mixer_layer(h_p, p_prompt[i], pos_p, zero_hist, prompt_attn, i, *params)
        h_s, ks_, vs_, cs_ = mixer_layer(h_s, p_sample[i], pos_s, state_conv[i], sample_attn, i, *params)
        nk_p.append(kp); nv_p.append(vp); nc_p.append(cp)
        nk_s.append(ks_); nv_s.append(vs_); nc_s.append(cs_)
    y_prompt = rms_norm(h_p, g_final)
    y_sample = rms_norm(h_s, g_final)
    new_k_prompt = jnp.stack(nk_p)
    new_v_prompt = jnp.stack(nv_p)
    new_conv_prompt = jnp.stack(nc_p)
    new_k_sample = jnp.stack(nk_s)
    new_v_sample = jnp.stack(nv_s)
    new_conv_sample = jnp.stack(nc_s)
    return (y_prompt, y_sample, new_k_prompt, new_v_prompt, new_conv_prompt, new_k_sample, new_v_sample, new_conv_sample)
```

```python
import functools
import math

import jax
import jax.numpy as jnp
from jax import lax
from jax.experimental import pallas as pl
from jax.experimental.pallas import tpu as pltpu

F32 = jnp.float32
BF16 = jnp.bfloat16

HEAD_DIM = 64
ROT_DIM = HEAD_DIM // 4
HALF_ROT = ROT_DIM // 2
ROPE_THETA = 500000.0
EPS = 1e-6
NEG_INF = -1e30

LANES = 128
SUBLANES = 8
VMEM_LIMIT = 56 * 1024 * 1024


def _sigmoid(x):
    return 1.0 / (1.0 + jnp.exp(-x))


def _silu(x):
    return x * _sigmoid(x)


def _rms(x, g):
    return x * lax.rsqrt(jnp.mean(x * x, axis=-1, keepdims=True) + EPS) * g


def _lam_init(layer_idx):
    return 0.8 - 0.6 * math.exp(-0.3 * layer_idx)


def _lam(lq1, lk1, lq2, lk2, lam_init):
    a = jnp.sum(lq1[...] * lk1[...], keepdims=True)
    b = jnp.sum(lq2[...] * lk2[...], keepdims=True)
    return jnp.exp(a) - jnp.exp(b) + lam_init


def _rope_tables(pos):
    inv = ROPE_THETA ** (-jnp.arange(0, ROT_DIM, 2, dtype=F32) / ROT_DIM)
    ang = pos.astype(F32)[:, None] * inv[None, :]
    d = jnp.arange(2 * HEAD_DIM) % HEAD_DIM
    f = d % HALF_ROT
    cos_f = jnp.cos(ang)[:, f]
    sin_f = jnp.sin(ang)[:, f]
    lo = (d < HALF_ROT)[None, :]
    hi = ((d >= HALF_ROT) & (d < ROT_DIM))[None, :]
    cos_t = jnp.where(lo | hi, cos_f, 1.0)
    s_next = jnp.where(lo, -sin_f, 0.0)
    s_prev = jnp.where(hi, sin_f, 0.0)
    return cos_t, s_next, s_prev


def _inproj_kernel(x_ref, gn_ref, w_ref, cos_ref, snext_ref, sprev_ref,
                   q_ref, k_ref, v_ref, sa_ref, g_ref, sc_ref, u_scr, ca_scr):
    n = pl.program_id(1)

    @pl.when(n == 0)
    def _norm():
        u_scr[...] = _rms(x_ref[...], gn_ref[...]).astype(BF16)

    z = jnp.dot(u_scr[...], w_ref[...], preferred_element_type=F32)

    def rope_to(out_ref, scale):
        cos_t, s_next, s_prev = cos_ref[...], snext_ref[...], sprev_ref[...]
        for h in range(z.shape[1] // LANES):
            zh = z[:, h * LANES:(h + 1) * LANES]
            nxt = pltpu.roll(zh, LANES - HALF_ROT, 1)
            prv = pltpu.roll(zh, HALF_ROT, 1)
            r = zh * cos_t + nxt * s_next + prv * s_prev
            if scale != 1.0:
                r = r * scale
            out_ref[:, h * LANES:(h + 1) * LANES] = r.astype(out_ref.dtype)

    @pl.when(n == 0)
    def _q():
        rope_to(q_ref, HEAD_DIM ** -0.5)

    @pl.when(n == 1)
    def _k():
        rope_to(k_ref, 1.0)

    @pl.when(n == 2)
    def _v():
        v_ref[...] = z

    @pl.when(n == 3)
    def _gate_a():
        sa_ref[...] = _silu(z).astype(sa_ref.dtype)

    @pl.when(n == 4)
    def _glu_a():
        ca_scr[...] = z

    @pl.when(n == 5)
    def _glu_b():
        g_ref[...] = ca_scr[...] * _sigmoid(z)

    @pl.when(n == 6)
    def _gate_c():
        sc_ref[...] = _silu(z).astype(sc_ref.dtype)


def _inproj(x2d, g_norm, w_in_bf, tables, tm, seg):
    m, d = x2d.shape
    n_seg = w_in_bf.shape[1] // seg
    assert n_seg == 7 and m % tm == 0
    cos_t, s_next, s_prev = tables
    t_blocks = cos_t.shape[0] // tm
    row = lambda i, n: (i, 0)
    tab = lambda i, n: (i % t_blocks, 0)
    out_sds = lambda dt: jax.ShapeDtypeStruct((m, seg), dt)
    return pl.pallas_call(
        _inproj_kernel,
        grid=(m // tm, n_seg),
        in_specs=[
            pl.BlockSpec((tm, d), row),
            pl.BlockSpec((1, d), lambda i, n: (0, 0)),
            pl.BlockSpec((d, seg), lambda i, n: (0, n)),
            pl.BlockSpec((tm, LANES), tab),
            pl.BlockSpec((tm, LANES), tab),
            pl.BlockSpec((tm, LANES), tab),
        ],
        out_specs=[pl.BlockSpec((tm, seg), row)] * 6,
        out_shape=[out_sds(BF16), out_sds(F32), out_sds(F32), out_sds(BF16), out_sds(F32), out_sds(BF16)],
        scratch_shapes=[pltpu.VMEM((tm, d), BF16), pltpu.VMEM((tm, seg), F32)],
        compiler_params=pltpu.CompilerParams(
            dimension_semantics=("parallel", "arbitrary"), vmem_limit_bytes=VMEM_LIMIT),
        name="inproj",
    )(x2d, g_norm.reshape(1, d), w_in_bf, cos_t, s_next, s_prev)


def _attn_kernel(lq1, lk1, lq2, lk2, gs_ref, q_ref, k_ref, v_ref, sa_ref, o_ref, kb_scr, vb_scr,
                 *, tq, lam_init):
    i = pl.program_id(2)

    @pl.when(i == 0)
    def _cast_kv():
        kb_scr[...] = k_ref[...].astype(BF16)
        vb_scr[...] = v_ref[...].astype(BF16)

    q = q_ref[...]
    lane = lax.broadcasted_iota(jnp.int32, q.shape, 1)
    qm = (jnp.where(lane < HEAD_DIM, q, jnp.zeros_like(q)),
          jnp.where(lane >= HEAD_DIM, q, jnp.zeros_like(q)))
    row = lax.broadcasted_iota(jnp.int32, (tq, tq), 0)
    col = lax.broadcasted_iota(jnp.int32, (tq, tq), 1)

    def kv_tile(j, carry, masked):
        off = pl.multiple_of(j * tq, tq)
        kt = kb_scr[pl.ds(off, tq), :]
        vt = vb_scr[pl.ds(off, tq), :]
        new = []
        for mi in range(2):
            m, l, acc = carry[mi]
            s = lax.dot_general(qm[mi], kt, (((1,), (1,)), ((), ())), preferred_element_type=F32)
            if masked:
                s = jnp.where(col <= row, s, NEG_INF)
            m_new = jnp.maximum(m, jnp.max(s, axis=-1, keepdims=True))
            alpha = jnp.exp(m - m_new)
            p = jnp.exp(s - m_new)
            l = alpha * l + jnp.sum(p, axis=-1, keepdims=True)
            acc = alpha * acc + jnp.dot(p.astype(BF16), vt, preferred_element_type=F32)
            new.append((m_new, l, acc))
        return tuple(new)

    init_one = (jnp.full((tq, 1), NEG_INF, F32), jnp.zeros((tq, 1), F32), jnp.zeros((tq, v_ref.shape[1]), F32))
    carry = lax.fori_loop(0, i, lambda j, c: kv_tile(j, c, False), (init_one, init_one))
    (m0, l0, a0), (m1, l1, a1) = kv_tile(i, carry, True)

    lam = _lam(lq1, lk1, lq2, lk2, lam_init)
    o = a0 / l0 - lam * (a1 / l1)
    o = _rms(o, gs_ref[...]) * (1.0 - lam_init)
    o_ref[...] = (o * sa_ref[...].astype(F32)).astype(o_ref.dtype)


def _prompt_attention(q, k, v, sa, lams, g_subln, tq, lam_init):
    b, s, aw = q.shape
    hw = g_subln.shape[-1]
    nh = aw // hw
    qmap = lambda bi, h, i: (bi, i, h)
    kvmap = lambda bi, h, i: (bi, 0, h)
    small = lambda bi, h, i: (0, 0)
    lam_specs = [pl.BlockSpec((1, HEAD_DIM), small)] * 4
    return pl.pallas_call(
        functools.partial(_attn_kernel, tq=tq, lam_init=lam_init),
        grid=(b, nh, s // tq),
        in_specs=lam_specs + [
            pl.BlockSpec((1, hw), small),
            pl.BlockSpec((None, tq, hw), qmap),
            pl.BlockSpec((None, s, hw), kvmap),
            pl.BlockSpec((None, s, hw), kvmap),
            pl.BlockSpec((None, tq, hw), qmap),
        ],
        out_specs=pl.BlockSpec((None, tq, hw), qmap),
        out_shape=jax.ShapeDtypeStruct((b, s, aw), BF16),
        scratch_shapes=[pltpu.VMEM((s, hw), BF16), pltpu.VMEM((s, hw), BF16)],
        compiler_params=pltpu.CompilerParams(
            dimension_semantics=("parallel", "parallel", "arbitrary"), vmem_limit_bytes=VMEM_LIMIT),
        name="prompt_attn",
    )(*lams, g_subln.reshape(1, hw), q, k, v, sa)


def _decode_kernel(pt_ref, lq1, lk1, lq2, lk2, gs_ref, q_ref, kn_ref, vn_ref, sa_ref, *rest,
                   pps, lam_init):
    k_refs, v_refs = rest[:pps], rest[pps:2 * pps]
    o_ref, m_scr, l_scr, acc_scr = rest[2 * pps:]
    p_idx = pl.program_id(1)
    nh, hw = q_ref.shape

    @pl.when(p_idx == 0)
    def _init():
        m_scr[...] = jnp.full(m_scr.shape, NEG_INF, F32)
        l_scr[...] = jnp.zeros(l_scr.shape, F32)
        acc_scr[...] = jnp.zeros(acc_scr.shape, F32)

    q = q_ref[...].astype(F32)
    d_map = lax.broadcasted_iota(jnp.int32, (hw, 2 * hw), 0) // HEAD_DIM
    n_map = lax.broadcasted_iota(jnp.int32, (hw, 2 * hw), 1) // hw
    sel = (d_map == n_map).astype(BF16)

    def scores(kp):
        rows = kp.shape[0]
        prod = (kp * q[None]).reshape(rows * nh, hw)
        s = jnp.dot(prod.astype(BF16), sel, preferred_element_type=F32)
        return s.reshape(rows, nh, 2 * hw)

    def update(s, vp):
        for mi in range(2):
            sm = s[:, :, mi * hw:(mi + 1) * hw]
            m_old = m_scr[mi]
            m_new = jnp.maximum(m_old, jnp.max(sm, axis=0))
            alpha = jnp.exp(m_old - m_new)
            p = jnp.exp(sm - m_new[None])
            l_scr[mi] = alpha * l_scr[mi] + jnp.sum(p, axis=0)
            acc_scr[mi] = alpha * acc_scr[mi] + jnp.sum(p * vp, axis=0)
            m_scr[mi] = m_new

    for t in range(pps):
        update(scores(k_refs[t][...]), v_refs[t][...])

    @pl.when(p_idx == pl.num_programs(1) - 1)
    def _finish():
        update(scores(kn_ref[...][None]), vn_ref[...][None])
        lam = _lam(lq1, lk1, lq2, lk2, lam_init)
        o = acc_scr[0] / l_scr[0] - lam * (acc_scr[1] / l_scr[1])
        o = _rms(o, gs_ref[...]) * (1.0 - lam_init)
        o_ref[...] = o * sa_ref[...].astype(F32)


def _decode_attention(q, k_new, v_new, sa, cache_k, cache_v, page_table, layer, lams, g_subln, pps, lam_init):
    db, nh, hw = q.shape
    n_pages = page_table.shape[1]
    page = cache_k.shape[2]
    assert n_pages % pps == 0
    tok = lambda bi, p, pt: (bi, 0, 0)
    small = lambda bi, p, pt: (0, 0)
    page_spec = lambda t: pl.BlockSpec(
        (None, None, page, nh, hw), lambda bi, p, pt: (layer, pt[bi, p * pps + t], 0, 0, 0))
    grid_spec = pltpu.PrefetchScalarGridSpec(
        num_scalar_prefetch=1,
        grid=(db, n_pages // pps),
        in_specs=[pl.BlockSpec((1, HEAD_DIM), small)] * 4 + [
            pl.BlockSpec((1, hw), small),
            pl.BlockSpec((None, nh, hw), tok),
            pl.BlockSpec((None, nh, hw), tok),
            pl.BlockSpec((None, nh, hw), tok),
            pl.BlockSpec((None, nh, hw), tok),
        ] + [page_spec(t) for t in range(pps)] * 2,
        out_specs=pl.BlockSpec((None, nh, hw), tok),
        scratch_shapes=[pltpu.VMEM((2, nh, hw), F32)] * 3,
    )
    return pl.pallas_call(
        functools.partial(_decode_kernel, pps=pps, lam_init=lam_init),
        grid_spec=grid_spec,
        out_shape=jax.ShapeDtypeStruct((db, nh, hw), F32),
        compiler_params=pltpu.CompilerParams(
            dimension_semantics=("parallel", "arbitrary"), vmem_limit_bytes=VMEM_LIMIT),
        name="decode_attn",
    )(page_table, *lams, g_subln.reshape(1, hw), q, k_new, v_new, sa,
      *([cache_k] * pps), *([cache_v] * pps))


def _conv_kernel(gc_ref, gh_ref, wdw_ref, bdw_ref, gcl_ref, bcl_ref, wpw_ref, sc_ref, o_ref,
                 buf, c_scr, *, ts, tc, halo, kw, rc):
    i = pl.program_id(1)
    c = pl.program_id(2)
    nc = c_scr.shape[0]
    buf[0:halo, :] = jnp.where(i == 0, 0.0, gh_ref[...])
    buf[halo:halo + ts, :] = gc_ref[...]
    off = halo - (kw - 1)
    for r in range(ts // rc):
        acc = jnp.broadcast_to(bdw_ref[...], (rc, tc))
        for j in range(kw):
            acc = acc + wdw_ref[j:j + 1, :] * buf[r * rc + off + j:r * rc + off + j + rc, :]
        c_scr[c, r * rc:(r + 1) * rc, :] = acc

    @pl.when(c == nc - 1)
    def _norm_pw():
        width = nc * tc
        cs = [c_scr[k] for k in range(nc)]
        mu = sum(jnp.sum(x, axis=-1, keepdims=True) for x in cs) / width
        var = sum(jnp.sum(jnp.square(x - mu), axis=-1, keepdims=True) for x in cs) / width
        inv = lax.rsqrt(var + EPS)
        acc = jnp.zeros(o_ref.shape, F32)
        for k in range(nc):
            y = (cs[k] - mu) * inv * gcl_ref[:, k * tc:(k + 1) * tc] + bcl_ref[:, k * tc:(k + 1) * tc]
            acc = acc + jnp.dot(_silu(y).astype(BF16), wpw_ref[k * tc:(k + 1) * tc, :],
                                preferred_element_type=F32)
        o_ref[...] = (acc * sc_ref[...].astype(F32)).astype(o_ref.dtype)


def _prompt_conv(g, sc, w_dw, b_dw, g_cln, b_cln, w_pw_bf, ts, tc):
    b, s, cw = g.shape
    kw = w_dw.shape[0]
    halo = -(-(kw - 1) // SUBLANES) * SUBLANES
    assert ts % halo == 0 and s % ts == 0 and cw % tc == 0
    hb = ts // halo
    vec = lambda a: a.reshape(1, cw)
    full = lambda bi, i, c: (0, 0)
    return pl.pallas_call(
        functools.partial(_conv_kernel, ts=ts, tc=tc, halo=halo, kw=kw, rc=64),
        grid=(b, s // ts, cw // tc),
        in_specs=[
            pl.BlockSpec((None, ts, tc), lambda bi, i, c: (bi, i, c)),
            pl.BlockSpec((None, halo, tc), lambda bi, i, c: (bi, jnp.maximum(i * hb - 1, 0), c)),
            pl.BlockSpec((kw, tc), lambda bi, i, c: (0, c)),
            pl.BlockSpec((1, tc), lambda bi, i, c: (0, c)),
            pl.BlockSpec((1, cw), full),
            pl.BlockSpec((1, cw), full),
            pl.BlockSpec((cw, cw), full),
            pl.BlockSpec((None, ts, cw), lambda bi, i, c: (bi, i, 0)),
        ],
        out_specs=pl.BlockSpec((None, ts, cw), lambda bi, i, c: (bi, i, 0)),
        out_shape=jax.ShapeDtypeStruct((b, s, cw), BF16),
        scratch_shapes=[pltpu.VMEM((halo + ts, tc), F32), pltpu.VMEM((cw // tc, ts, tc), F32)],
        compiler_params=pltpu.CompilerParams(
            dimension_semantics=("parallel", "parallel", "arbitrary"), vmem_limit_bytes=VMEM_LIMIT),
        name="prompt_conv",
    )(g, g, w_dw, vec(b_dw), vec(g_cln), vec(b_cln), w_pw_bf, sc)


def _decode_conv_kernel(hist_ref, wdw_ref, bdw_ref, gcl_ref, bcl_ref, wpw_ref, sc_ref, o_ref):
    kw = hist_ref.shape[0]
    acc = jnp.broadcast_to(bdw_ref[...], o_ref.shape)
    for j in range(kw):
        acc = acc + wdw_ref[j:j + 1, :] * hist_ref[j]
    mu = jnp.mean(acc, axis=-1, keepdims=True)
    var = jnp.mean(jnp.square(acc - mu), axis=-1, keepdims=True)
    y = (acc - mu) * lax.rsqrt(var + EPS) * gcl_ref[...] + bcl_ref[...]
    c = jnp.dot(_silu(y).astype(BF16), wpw_ref[...], preferred_element_type=F32)
    o_ref[...] = (c * sc_ref[...].astype(F32)).astype(o_ref.dtype)


def _decode_conv(hist_t, sc, w_dw, b_dw, g_cln, b_cln, w_pw_bf):
    kw, db, cw = hist_t.shape
    vec = lambda a: a.reshape(1, cw)
    return pl.pallas_call(
        _decode_conv_kernel,
        out_shape=jax.ShapeDtypeStruct((db, cw), BF16),
        compiler_params=pltpu.CompilerParams(vmem_limit_bytes=VMEM_LIMIT),
        name="decode_conv",
    )(hist_t, w_dw, vec(b_dw), vec(g_cln), vec(b_cln), w_pw_bf, sc)


def _out_kernel(x_ref, oa_ref, oc_ref, p_ref, wout_ref, gple_ref, wpg_ref, wple_ref, gfin_ref, y_ref,
                *, final_norm):
    aw = oa_ref.shape[1]
    h = x_ref[...]
    h = h + jnp.dot(oa_ref[...], wout_ref[0:aw, :], preferred_element_type=F32)
    h = h + jnp.dot(oc_ref[...], wout_ref[aw:, :], preferred_element_type=F32)
    r = _rms(h, gple_ref[...]).astype(BF16)
    gate = _sigmoid(jnp.dot(r, wpg_ref[...], preferred_element_type=F32))
    e = jnp.dot(p_ref[...].astype(BF16), wple_ref[...], preferred_element_type=F32)
    h = h + gate * e
    y_ref[...] = _rms(h, gfin_ref[...]) if final_norm else h


def _outproj(x2d, oa, oc, p2d, w_out_bf, g_ple, w_pg_bf, w_ple_bf, g_final, tm, final_norm):
    m, d = x2d.shape
    aw, cw, pd = oa.shape[1], oc.shape[1], p2d.shape[1]
    row = lambda i: (i, 0)
    full = lambda i: (0, 0)
    once = pl.Buffered(1)
    return pl.pallas_call(
        functools.partial(_out_kernel, final_norm=final_norm),
        grid=(m // tm,),
        in_specs=[
            pl.BlockSpec((tm, d), row),
            pl.BlockSpec((tm, aw), row),
            pl.BlockSpec((tm, cw), row),
            pl.BlockSpec((tm, pd), row),
            pl.BlockSpec((aw + cw, d), full, pipeline_mode=once),
            pl.BlockSpec((1, d), full),
            pl.BlockSpec((d, d), full, pipeline_mode=once),
            pl.BlockSpec((pd, d), full, pipeline_mode=once),
            pl.BlockSpec((1, d), full),
        ],
        out_specs=pl.BlockSpec((tm, d), row),
        out_shape=jax.ShapeDtypeStruct((m, d), F32),
        compiler_params=pltpu.CompilerParams(
            dimension_semantics=("parallel",), vmem_limit_bytes=VMEM_LIMIT),
        name="outproj",
    )(x2d, oa, oc, p2d, w_out_bf, g_ple.reshape(1, d), w_pg_bf, w_ple_bf, g_final.reshape(1, d))


def kernel(x_prompt, x_sample, p_prompt, p_sample, cache_k, cache_v, state_conv, page_table, w_in, g_norm, lam_q1, lam_k1, lam_q2, lam_k2, g_subln, w_dw, b_dw, g_cln, b_cln, w_pw, w_out, g_ple, w_pg, w_ple, g_final):
    b, s, d = x_prompt.shape
    db, ds, _ = x_sample.shape
    depth = w_in.shape[0]
    nh, hw = cache_k.shape[3], cache_k.shape[4]
    aw = nh * hw
    cw = w_dw.shape[2]
    kw = w_dw.shape[1]
    past = page_table.shape[1] * cache_k.shape[2]
    assert ds == 1 and aw == cw and hw == 2 * HEAD_DIM

    tab_p = _rope_tables(jnp.arange(s, dtype=jnp.int32))
    tab_s = _rope_tables(jnp.full((db * ds,), past, dtype=jnp.int32))

    h_p = x_prompt.reshape(b * s, d)
    h_s = x_sample.reshape(db * ds, d)
    outs = {k: [] for k in ("kp", "vp", "cp", "ks", "vs", "cs")}
    for i in range(depth):
        lam_init = _lam_init(i)
        last = i == depth - 1
        lams = tuple(a[i].reshape(1, HEAD_DIM) for a in (lam_q1, lam_k1, lam_q2, lam_k2))
        w_in_bf, w_pw_bf = w_in[i].astype(BF16), w_pw[i].astype(BF16)
        w_out_bf, w_pg_bf, w_ple_bf = w_out[i].astype(BF16), w_pg[i].astype(BF16), w_ple[i].astype(BF16)

        q, k, v, sa, g, sc = _inproj(h_p, g_norm[i], w_in_bf, tab_p, tm=512, seg=aw)
        as3 = lambda a: a.reshape(b, s, -1)
        oa = _prompt_attention(as3(q), as3(k), as3(v), as3(sa), lams, g_subln[i], tq=256, lam_init=lam_init)
        oc = _prompt_conv(as3(g), as3(sc), w_dw[i], b_dw[i], g_cln[i], b_cln[i], w_pw_bf, ts=256, tc=256)
        h_p = _outproj(h_p, oa.reshape(b * s, aw), oc.reshape(b * s, cw), p_prompt[i].reshape(b * s, -1),
                       w_out_bf, g_ple[i], w_pg_bf, w_ple_bf, g_final, tm=256, final_norm=last)
        outs["kp"].append(k.reshape(b, s, nh, hw))
        outs["vp"].append(v.reshape(b, s, nh, hw))
        outs["cp"].append(as3(g)[:, s - (kw - 1):])

        q, k, v, sa, g, sc = _inproj(h_s, g_norm[i], w_in_bf, tab_s, tm=db * ds, seg=aw)
        hd3 = lambda a: a.reshape(db, nh, hw)
        oa = _decode_attention(hd3(q), hd3(k), hd3(v), hd3(sa), cache_k, cache_v, page_table, i, lams,
                               g_subln[i], pps=4, lam_init=lam_init)
        hist = jnp.concatenate([state_conv[i], g.reshape(db, ds, cw)], axis=1)
        oc = _decode_conv(hist.transpose(1, 0, 2), sc, w_dw[i], b_dw[i], g_cln[i], b_cln[i], w_pw_bf)
        h_s = _outproj(h_s, oa.reshape(db, aw).astype(BF16), oc, p_sample[i].reshape(db * ds, -1),
                       w_out_bf, g_ple[i], w_pg_bf, w_ple_bf, g_final, tm=db * ds, final_norm=last)
        outs["ks"].append(k.reshape(db, ds, nh, hw))
        outs["vs"].append(v.reshape(db, ds, nh, hw))
        outs["cs"].append(hist[:, 1:])

    st = lambda key: jnp.stack(outs[key])
    return (h_p.reshape(b, s, d), h_s.reshape(db, ds, d),
            st("kp"), st("vp"), st("cp"), st("ks"), st("vs"), st("cs"))
```

```python
import functools
import math

import jax
import jax.numpy as jnp
from jax import lax
from jax.experimental import pallas as pl
from jax.experimental.pallas import tpu as pltpu

F32 = jnp.float32
BF16 = jnp.bfloat16

HEAD_DIM = 64
ROT_DIM = HEAD_DIM // 4
HALF_ROT = ROT_DIM // 2
ROPE_THETA = 500000.0
EPS = 1e-6
NEG_INF = -1e30
Q_SCALE = HEAD_DIM ** -0.5 * math.log2(math.e)

LANES = 128
SUBLANES = 8
VMEM_LIMIT = 56 * 1024 * 1024


def _sigmoid(x):
    return 1.0 / (1.0 + jnp.exp(-x))


def _silu(x):
    return x * _sigmoid(x)


def _rms(x, g):
    return x * lax.rsqrt(jnp.mean(x * x, axis=-1, keepdims=True) + EPS) * g


def _lam_init(layer_idx):
    return 0.8 - 0.6 * math.exp(-0.3 * layer_idx)


def _lam(lq1, lk1, lq2, lk2, lam_init):
    a = jnp.sum(lq1[...] * lk1[...], keepdims=True)
    b = jnp.sum(lq2[...] * lk2[...], keepdims=True)
    return jnp.exp(a) - jnp.exp(b) + lam_init


def _rope_tables(pos):
    inv = ROPE_THETA ** (-jnp.arange(0, ROT_DIM, 2, dtype=F32) / ROT_DIM)
    ang = pos.astype(F32)[:, None] * inv[None, :]
    d = jnp.arange(2 * HEAD_DIM) % HEAD_DIM
    f = d % HALF_ROT
    cos_f = jnp.cos(ang)[:, f]
    sin_f = jnp.sin(ang)[:, f]
    lo = (d < HALF_ROT)[None, :]
    hi = ((d >= HALF_ROT) & (d < ROT_DIM))[None, :]
    cos_t = jnp.where(lo | hi, cos_f, 1.0)
    s_next = jnp.where(lo, -sin_f, 0.0)
    s_prev = jnp.where(hi, sin_f, 0.0)
    return cos_t, s_next, s_prev


def _inproj_kernel(x_ref, gn_ref, w_ref, cos_ref, snext_ref, sprev_ref,
                   q_ref, k_ref, v_ref, sa_ref, g_ref, sc_ref, u_scr, ca_scr):
    n = pl.program_id(1)

    @pl.when(n == 0)
    def _norm():
        u_scr[...] = _rms(x_ref[...], gn_ref[...]).astype(BF16)

    z = jnp.dot(u_scr[...], w_ref[...], preferred_element_type=F32)

    def rope_to(out_ref, scale):
        cos_t, s_next, s_prev = cos_ref[...], snext_ref[...], sprev_ref[...]
        for h in range(z.shape[1] // LANES):
            zh = z[:, h * LANES:(h + 1) * LANES]
            nxt = pltpu.roll(zh, LANES - HALF_ROT, 1)
            prv = pltpu.roll(zh, HALF_ROT, 1)
            r = zh * cos_t + nxt * s_next + prv * s_prev
            if scale != 1.0:
                r = r * scale
            out_ref[:, h * LANES:(h + 1) * LANES] = r.astype(out_ref.dtype)

    @pl.when(n == 0)
    def _q():
        rope_to(q_ref, Q_SCALE)

    @pl.when(n == 1)
    def _k():
        rope_to(k_ref, 1.0)

    @pl.when(n == 2)
    def _v():
        v_ref[...] = z

    @pl.when(n == 3)
    def _gate_a():
        sa_ref[...] = _silu(z).astype(sa_ref.dtype)

    @pl.when(n == 4)
    def _glu_a():
        ca_scr[...] = z

    @pl.when(n == 5)
    def _glu_b():
        g_ref[...] = ca_scr[...] * _sigmoid(z)

    @pl.when(n == 6)
    def _gate_c():
        sc_ref[...] = _silu(z).astype(sc_ref.dtype)


def _inproj(x2d, g_norm, w_in_bf, tables, tm, seg):
    m, d = x2d.shape
    n_seg = w_in_bf.shape[1] // seg
    assert n_seg == 7 and m % tm == 0
    cos_t, s_next, s_prev = tables
    t_blocks = cos_t.shape[0] // tm
    row = lambda i, n: (i, 0)
    tab = lambda i, n: (i % t_blocks, 0)
    out_sds = lambda dt: jax.ShapeDtypeStruct((m, seg), dt)
    return pl.pallas_call(
        _inproj_kernel,
        grid=(m // tm, n_seg),
        in_specs=[
            pl.BlockSpec((tm, d), row),
            pl.BlockSpec((1, d), lambda i, n: (0, 0)),
            pl.BlockSpec((d, seg), lambda i, n: (0, n)),
            pl.BlockSpec((tm, LANES), tab),
            pl.BlockSpec((tm, LANES), tab),
            pl.BlockSpec((tm, LANES), tab),
        ],
        out_specs=[pl.BlockSpec((tm, seg), row)] * 6,
        out_shape=[out_sds(BF16), out_sds(F32), out_sds(F32), out_sds(BF16), out_sds(F32), out_sds(BF16)],
        scratch_shapes=[pltpu.VMEM((tm, d), BF16), pltpu.VMEM((tm, seg), F32)],
        compiler_params=pltpu.CompilerParams(
            dimension_semantics=("parallel", "arbitrary"), vmem_limit_bytes=VMEM_LIMIT),
        name="inproj",
    )(x2d, g_norm.reshape(1, d), w_in_bf, cos_t, s_next, s_prev)


def _attn_kernel(lq1, lk1, lq2, lk2, gs_ref, q_ref, k_ref, v_ref, sa_ref, o_ref, kt_scr, vb_scr,
                 *, tq, lam_init):
    s_len, hw = k_ref.shape
    kt_scr[...] = k_ref[...].T.astype(BF16)
    vb_scr[:, 0:hw] = v_ref[...].astype(BF16)
    vb_scr[:, hw:2 * hw] = jnp.ones((s_len, hw), BF16)

    lam = _lam(lq1, lk1, lq2, lk2, lam_init)
    gs = gs_ref[...]
    lane = lax.broadcasted_iota(jnp.int32, (tq, hw), 1)
    q_row = lax.broadcasted_iota(jnp.int32, (2 * tq, tq), 0) % tq
    k_col = lax.broadcasted_iota(jnp.int32, (2 * tq, tq), 1)
    causal = k_col <= q_row

    for i in range(s_len // tq):
        r0 = i * tq
        q = q_ref[r0:r0 + tq, :]
        zero = jnp.zeros_like(q)
        q2 = jnp.concatenate([jnp.where(lane < HEAD_DIM, q, zero), jnp.where(lane >= HEAD_DIM, q, zero)], axis=0)
        m = jnp.full((2 * tq, 1), NEG_INF, F32)
        acc = jnp.zeros((2 * tq, 2 * hw), F32)
        for j in range(i + 1):
            c0 = j * tq
            s = jnp.dot(q2, kt_scr[:, c0:c0 + tq], preferred_element_type=F32)
            if j == i:
                s = jnp.where(causal, s, NEG_INF)
            m_new = jnp.maximum(m, jnp.max(s, axis=-1, keepdims=True))
            alpha = jnp.exp2(m - m_new)
            p = jnp.exp2(s - m_new).astype(BF16)
            acc = alpha * acc + jnp.dot(p, vb_scr[c0:c0 + tq, :], preferred_element_type=F32)
            m = m_new
        o = acc[0:tq, 0:hw] / acc[0:tq, hw:] - lam * (acc[tq:, 0:hw] / acc[tq:, hw:])
        o = _rms(o, gs) * (1.0 - lam_init)
        o_ref[r0:r0 + tq, :] = (o * sa_ref[r0:r0 + tq, :].astype(F32)).astype(o_ref.dtype)


def _prompt_attention(q, k, v, sa, lams, g_subln, tq, lam_init):
    b, s, aw = q.shape
    hw = g_subln.shape[-1]
    nh = aw // hw
    head = lambda bi, h: (bi, 0, h)
    small = lambda bi, h: (0, 0)
    head_spec = pl.BlockSpec((None, s, hw), head)
    return pl.pallas_call(
        functools.partial(_attn_kernel, tq=tq, lam_init=lam_init),
        grid=(b, nh),
        in_specs=[pl.BlockSpec((1, HEAD_DIM), small)] * 4 + [pl.BlockSpec((1, hw), small)] + [head_spec] * 4,
        out_specs=head_spec,
        out_shape=jax.ShapeDtypeStruct((b, s, aw), BF16),
        scratch_shapes=[pltpu.VMEM((hw, s), BF16), pltpu.VMEM((s, 2 * hw), BF16)],
        compiler_params=pltpu.CompilerParams(
            dimension_semantics=("parallel", "parallel"), vmem_limit_bytes=VMEM_LIMIT),
        name="prompt_attn",
    )(*lams, g_subln.reshape(1, hw), q, k, v, sa)


def _decode_kernel(pt_ref, lq1, lk1, lq2, lk2, gs_ref, q_ref, kn_ref, vn_ref, sa_ref, *rest,
                   pps, lam_init):
    k_refs, v_refs = rest[:pps], rest[pps:2 * pps]
    o_ref, m_scr, l_scr, acca_scr, accb_scr = rest[2 * pps:]
    p_idx = pl.program_id(1)
    nh, hw = q_ref.shape

    @pl.when(p_idx == 0)
    def _init():
        m_scr[...] = jnp.full(m_scr.shape, NEG_INF, F32)
        l_scr[...] = jnp.zeros(l_scr.shape, F32)
        acca_scr[...] = jnp.zeros(acca_scr.shape, F32)
        accb_scr[...] = jnp.zeros(accb_scr.shape, F32)

    q = q_ref[...].astype(F32)
    d_map = lax.broadcasted_iota(jnp.int32, (hw, hw), 0) // HEAD_DIM
    n_map = lax.broadcasted_iota(jnp.int32, (hw, hw), 1) // HEAD_DIM
    sel = (d_map == n_map).astype(BF16)

    def swap_maps(x):
        return pltpu.roll(x, HEAD_DIM, x.ndim - 1)

    def update(t, kp, vp):
        rows = kp.shape[0]
        prod = (kp * q[None]).reshape(rows * nh, hw)
        s = jnp.dot(prod.astype(BF16), sel, preferred_element_type=F32)
        m_old = m_scr[t]
        m_new = jnp.maximum(m_old, jnp.max(s.reshape(rows, nh, hw), axis=0))
        alpha = jnp.exp2(m_old - m_new)
        p2 = jnp.exp2(s.reshape(rows, nh, hw) - m_new[None]).reshape(rows * nh, hw)
        p = p2.reshape(rows, nh, hw)
        p_sw = swap_maps(p2).reshape(rows, nh, hw)
        l_scr[t] = alpha * l_scr[t] + jnp.sum(p, axis=0)
        acca_scr[t] = alpha * acca_scr[t] + jnp.sum(p * vp, axis=0)
        accb_scr[t] = swap_maps(alpha) * accb_scr[t] + jnp.sum(p_sw * vp, axis=0)
        m_scr[t] = m_new

    for t in range(pps):
        update(t, k_refs[t][...], v_refs[t][...])

    @pl.when(p_idx == pl.num_programs(1) - 1)
    def _finish():
        update(0, kn_ref[...][None], vn_ref[...][None])
        m_all = m_scr[...]
        m = jnp.max(m_all, axis=0)
        w = jnp.exp2(m_all - m[None])
        l = jnp.sum(w * l_scr[...], axis=0)
        acc_a = jnp.sum(w * acca_scr[...], axis=0)
        acc_b = jnp.sum(swap_maps(w.reshape(pps * nh, hw)).reshape(pps, nh, hw) * accb_scr[...], axis=0)
        lam = _lam(lq1, lk1, lq2, lk2, lam_init)
        na = acc_a / l
        nb = acc_b / swap_maps(l)
        lo = lax.broadcasted_iota(jnp.int32, (nh, hw), 1) < HEAD_DIM
        o = jnp.where(lo, na, nb) - lam * jnp.where(lo, nb, na)
        o = _rms(o, gs_ref[...]) * (1.0 - lam_init)
        o_ref[...] = o * sa_ref[...].astype(F32)


def _decode_attention(q, k_new, v_new, sa, cache_k, cache_v, page_table, layer, lams, g_subln, pps, lam_init):
    db, nh, hw = q.shape
    n_pages = page_table.shape[1]
    page = cache_k.shape[2]
    assert n_pages % pps == 0
    tok = lambda bi, p, pt: (bi, 0, 0)
    small = lambda bi, p, pt: (0, 0)
    page_spec = lambda t: pl.BlockSpec(
        (None, None, page, nh, hw), lambda bi, p, pt: (layer, pt[bi, p * pps + t], 0, 0, 0))
    grid_spec = pltpu.PrefetchScalarGridSpec(
        num_scalar_prefetch=1,
        grid=(db, n_pages // pps),
        in_specs=[pl.BlockSpec((1, HEAD_DIM), small)] * 4 + [
            pl.BlockSpec((1, hw), small),
            pl.BlockSpec((None, nh, hw), tok),
            pl.BlockSpec((None, nh, hw), tok),
            pl.BlockSpec((None, nh, hw), tok),
            pl.BlockSpec((None, nh, hw), tok),
        ] + [page_spec(t) for t in range(pps)] * 2,
        out_specs=pl.BlockSpec((None, nh, hw), tok),
        scratch_shapes=[pltpu.VMEM((pps, nh, hw), F32)] * 4,
    )
    return pl.pallas_call(
        functools.partial(_decode_kernel, pps=pps, lam_init=lam_init),
        grid_spec=grid_spec,
        out_shape=jax.ShapeDtypeStruct((db, nh, hw), F32),
        compiler_params=pltpu.CompilerParams(
            dimension_semantics=("parallel", "arbitrary"), vmem_limit_bytes=VMEM_LIMIT),
        name="decode_attn",
    )(page_table, *lams, g_subln.reshape(1, hw), q, k_new, v_new, sa,
      *([cache_k] * pps), *([cache_v] * pps))


def _conv_kernel(gc_ref, gh_ref, wdw_ref, bdw_ref, gcl_ref, bcl_ref, wpw_ref, sc_ref, o_ref,
                 sh_scr, c_scr, *, ts, tc, halo, kw, rc):
    i = pl.program_id(1)
    c = pl.program_id(2)
    nc = c_scr.shape[0]
    off = halo - (kw - 1)
    n_copy = halo + ts - SUBLANES
    sh_scr[0, 0:halo, :] = jnp.where(i == 0, 0.0, gh_ref[...])
    sh_scr[0, halo:halo + ts, :] = gc_ref[...]
    for ph in range(1, SUBLANES):
        sh_scr[ph, 0:n_copy, :] = sh_scr[0, ph:ph + n_copy, :]
    for r in range(ts // rc):
        acc = jnp.broadcast_to(bdw_ref[...], (rc, tc))
        for j in range(kw):
            ph, base = (off + j) % SUBLANES, (off + j) // SUBLANES * SUBLANES
            acc = acc + wdw_ref[j:j + 1, :] * sh_scr[ph, r * rc + base:r * rc + base + rc, :]
        c_scr[c, r * rc:(r + 1) * rc, :] = acc

    @pl.when(c == nc - 1)
    def _norm_pw():
        width = nc * tc
        cs = [c_scr[k] for k in range(nc)]
        mu = sum(jnp.sum(x, axis=-1, keepdims=True) for x in cs) / width
        var = sum(jnp.sum(jnp.square(x - mu), axis=-1, keepdims=True) for x in cs) / width
        inv = lax.rsqrt(var + EPS)
        acc = jnp.zeros(o_ref.shape, F32)
        for k in range(nc):
            y = (cs[k] - mu) * inv * gcl_ref[:, k * tc:(k + 1) * tc] + bcl_ref[:, k * tc:(k + 1) * tc]
            acc = acc + jnp.dot(_silu(y).astype(BF16), wpw_ref[k * tc:(k + 1) * tc, :],
                                preferred_element_type=F32)
        o_ref[...] = (acc * sc_ref[...].astype(F32)).astype(o_ref.dtype)


def _prompt_conv(g, sc, w_dw, b_dw, g_cln, b_cln, w_pw_bf, ts, tc):
    b, s, cw = g.shape
    kw = w_dw.shape[0]
    halo = -(-(kw - 1) // SUBLANES) * SUBLANES
    assert ts % halo == 0 and s % ts == 0 and cw % tc == 0
    hb = ts // halo
    vec = lambda a: a.reshape(1, cw)
    full = lambda bi, i, c: (0, 0)
    return pl.pallas_call(
        functools.partial(_conv_kernel, ts=ts, tc=tc, halo=halo, kw=kw, rc=64),
        grid=(b, s // ts, cw // tc),
        in_specs=[
            pl.BlockSpec((None, ts, tc), lambda bi, i, c: (bi, i, c)),
            pl.BlockSpec((None, halo, tc), lambda bi, i, c: (bi, jnp.maximum(i * hb - 1, 0), c)),
            pl.BlockSpec((kw, tc), lambda bi, i, c: (0, c)),
            pl.BlockSpec((1, tc), lambda bi, i, c: (0, c)),
            pl.BlockSpec((1, cw), full),
            pl.BlockSpec((1, cw), full),
            pl.BlockSpec((cw, cw), full),
            pl.BlockSpec((None, ts, cw), lambda bi, i, c: (bi, i, 0)),
        ],
        out_specs=pl.BlockSpec((None, ts, cw), lambda bi, i, c: (bi, i, 0)),
        out_shape=jax.ShapeDtypeStruct((b, s, cw), BF16),
        scratch_shapes=[pltpu.VMEM((SUBLANES, halo + ts, tc), F32),
                        pltpu.VMEM((cw // tc, ts, tc), F32)],
        compiler_params=pltpu.CompilerParams(
            dimension_semantics=("parallel", "parallel", "arbitrary"), vmem_limit_bytes=VMEM_LIMIT),
        name="prompt_conv",
    )(g, g, w_dw, vec(b_dw), vec(g_cln), vec(b_cln), w_pw_bf, sc)


def _decode_conv_kernel(hist_ref, wdw_ref, bdw_ref, gcl_ref, bcl_ref, wpw_ref, sc_ref, o_ref):
    kw = hist_ref.shape[0]
    acc = jnp.broadcast_to(bdw_ref[...], o_ref.shape)
    for j in range(kw):
        acc = acc + wdw_ref[j:j + 1, :] * hist_ref[j]
    mu = jnp.mean(acc, axis=-1, keepdims=True)
    var = jnp.mean(jnp.square(acc - mu), axis=-1, keepdims=True)
    y = (acc - mu) * lax.rsqrt(var + EPS) * gcl_ref[...] + bcl_ref[...]
    c = jnp.dot(_silu(y).astype(BF16), wpw_ref[...], preferred_element_type=F32)
    o_ref[...] = (c * sc_ref[...].astype(F32)).astype(o_ref.dtype)


def _decode_conv(hist_t, sc, w_dw, b_dw, g_cln, b_cln, w_pw_bf):
    kw, db, cw = hist_t.shape
    vec = lambda a: a.reshape(1, cw)
    return pl.pallas_call(
        _decode_conv_kernel,
        out_shape=jax.ShapeDtypeStruct((db, cw), BF16),
        compiler_params=pltpu.CompilerParams(vmem_limit_bytes=VMEM_LIMIT),
        name="decode_conv",
    )(hist_t, w_dw, vec(b_dw), vec(g_cln), vec(b_cln), w_pw_bf, sc)


def _out_kernel(x_ref, oa_ref, oc_ref, p_ref, wout_ref, gple_ref, wpg_ref, wple_ref, gfin_ref, y_ref,
                *, final_norm):
    aw = oa_ref.shape[1]
    h = x_ref[...]
    h = h + jnp.dot(oa_ref[...], wout_ref[0:aw, :], preferred_element_type=F32)
    h = h + jnp.dot(oc_ref[...], wout_ref[aw:, :], preferred_element_type=F32)
    r = _rms(h, gple_ref[...]).astype(BF16)
    gate = _sigmoid(jnp.dot(r, wpg_ref[...], preferred_element_type=F32))
    e = jnp.dot(p_ref[...].astype(BF16), wple_ref[...], preferred_element_type=F32)
    h = h + gate * e
    y_ref[...] = _rms(h, gfin_ref[...]) if final_norm else h


def _outproj(x2d, oa, oc, p2d, w_out_bf, g_ple, w_pg_bf, w_ple_bf, g_final, tm, final_norm):
    m, d = x2d.shape
    aw, cw, pd = oa.shape[1], oc.shape[1], p2d.shape[1]
    row = lambda i: (i, 0)
    full = lambda i: (0, 0)
    once = pl.Buffered(1)
    return pl.pallas_call(
        functools.partial(_out_kernel, final_norm=final_norm),
        grid=(m // tm,),
        in_specs=[
            pl.BlockSpec((tm, d), row),
            pl.BlockSpec((tm, aw), row),
            pl.BlockSpec((tm, cw), row),
            pl.BlockSpec((tm, pd), row),
            pl.BlockSpec((aw + cw, d), full, pipeline_mode=once),
            pl.BlockSpec((1, d), full),
            pl.BlockSpec((d, d), full, pipeline_mode=once),
            pl.BlockSpec((pd, d), full, pipeline_mode=once),
            pl.BlockSpec((1, d), full),
        ],
        out_specs=pl.BlockSpec((tm, d), row),
        out_shape=jax.ShapeDtypeStruct((m, d), F32),
        compiler_params=pltpu.CompilerParams(
            dimension_semantics=("parallel",), vmem_limit_bytes=VMEM_LIMIT),
        name="outproj",
    )(x2d, oa, oc, p2d, w_out_bf, g_ple.reshape(1, d), w_pg_bf, w_ple_bf, g_final.reshape(1, d))


def kernel(x_prompt, x_sample, p_prompt, p_sample, cache_k, cache_v, state_conv, page_table, w_in, g_norm, lam_q1, lam_k1, lam_q2, lam_k2, g_subln, w_dw, b_dw, g_cln, b_cln, w_pw, w_out, g_ple, w_pg, w_ple, g_final):
    b, s, d = x_prompt.shape
    db, ds, _ = x_sample.shape
    depth = w_in.shape[0]
    nh, hw = cache_k.shape[3], cache_k.shape[4]
    aw = nh * hw
    cw = w_dw.shape[2]
    kw = w_dw.shape[1]
    past = page_table.shape[1] * cache_k.shape[2]
    assert ds == 1 and aw == cw and hw == 2 * HEAD_DIM

    tab_p = _rope_tables(jnp.arange(s, dtype=jnp.int32))
    tab_s = _rope_tables(jnp.full((db * ds,), past, dtype=jnp.int32))

    h_p = x_prompt.reshape(b * s, d)
    h_s = x_sample.reshape(db * ds, d)
    outs = {k: [] for k in ("kp", "vp", "cp", "ks", "vs", "cs")}
    for i in range(depth):
        lam_init = _lam_init(i)
        last = i == depth - 1
        lams = tuple(a[i].reshape(1, HEAD_DIM) for a in (lam_q1, lam_k1, lam_q2, lam_k2))
        w_in_bf, w_pw_bf = w_in[i].astype(BF16), w_pw[i].astype(BF16)
        w_out_bf, w_pg_bf, w_ple_bf = w_out[i].astype(BF16), w_pg[i].astype(BF16), w_ple[i].astype(BF16)

        q, k, v, sa, g, sc = _inproj(h_p, g_norm[i], w_in_bf, tab_p, tm=512, seg=aw)
        as3 = lambda a: a.reshape(b, s, -1)
        oa = _prompt_attention(as3(q), as3(k), as3(v), as3(sa), lams, g_subln[i], tq=256, lam_init=lam_init)
        oc = _prompt_conv(as3(g), as3(sc), w_dw[i], b_dw[i], g_cln[i], b_cln[i], w_pw_bf, ts=256, tc=256)
        h_p = _outproj(h_p, oa.reshape(b * s, aw), oc.reshape(b * s, cw), p_prompt[i].reshape(b * s, -1),
                       w_out_bf, g_ple[i], w_pg_bf, w_ple_bf, g_final, tm=256, final_norm=last)
        outs["kp"].append(k.reshape(b, s, nh, hw))
        outs["vp"].append(v.reshape(b, s, nh, hw))
        outs["cp"].append(as3(g)[:, s - (kw - 1):])

        q, k, v, sa, g, sc = _inproj(h_s, g_norm[i], w_in_bf, tab_s, tm=db * ds, seg=aw)
        hd3 = lambda a: a.reshape(db, nh, hw)
        oa = _decode_attention(hd3(q), hd3(k), hd3(v), hd3(sa), cache_k, cache_v, page_table, i, lams,
                               g_subln[i], pps=8, lam_init=lam_init)
        hist = jnp.concatenate([state_conv[i], g.reshape(db, ds, cw)], axis=1)
        oc = _decode_conv(hist.transpose(1, 0, 2), sc, w_dw[i], b_dw[i], g_cln[i], b_cln[i], w_pw_bf)
        h_s = _outproj(h_s, oa.reshape(db, aw).astype(BF16), oc, p_sample[i].reshape(db * ds, -1),
                       w_out_bf, g_ple[i], w_pg_bf, w_ple_bf, g_final, tm=db * ds, final_norm=last)
        outs["ks"].append(k.reshape(db, ds, nh, hw))
        outs["vs"].append(v.reshape(db, ds, nh, hw))
        outs["cs"].append(hist[:, 1:])

    st = lambda key: jnp.stack(outs[key])
    return (h_p.reshape(b, s, d), h_s.reshape(db, ds, d),
            st("kp"), st("vp"), st("cp"), st("ks"), st("vs"), st("cs"))
```

```python
import functools
import math

import jax
import jax.numpy as jnp
from jax import lax
from jax.experimental import pallas as pl
from jax.experimental.pallas import tpu as pltpu

F32 = jnp.float32
BF16 = jnp.bfloat16

HEAD_DIM = 64
ROT_DIM = HEAD_DIM // 4
HALF_ROT = ROT_DIM // 2
ROPE_THETA = 500000.0
EPS = 1e-6
NEG_INF = -1e30
Q_SCALE = HEAD_DIM ** -0.5 * math.log2(math.e)

LANES = 128
SUBLANES = 8
VMEM_LIMIT = 56 * 1024 * 1024
MXU_DIM = 256
INPROJ_CHUNK = MXU_DIM
OUT_SUBTILE = 256
N_SLOTS = 8
N_HALVES = 2


def _sigmoid(x):
    return 1.0 / (1.0 + jnp.exp(-x))


def _silu(x):
    return x * _sigmoid(x)


def _rms(x, g):
    return x * lax.rsqrt(jnp.mean(x * x, axis=-1, keepdims=True) + EPS) * g


def _lam_init(layer_idx):
    return 0.8 - 0.6 * math.exp(-0.3 * layer_idx)


def _lam(lq1, lk1, lq2, lk2, lam_init):
    a = jnp.sum(lq1[...] * lk1[...], keepdims=True)
    b = jnp.sum(lq2[...] * lk2[...], keepdims=True)
    return jnp.exp(a) - jnp.exp(b) + lam_init


def _rope_tables(pos):
    inv = ROPE_THETA ** (-jnp.arange(0, ROT_DIM, 2, dtype=F32) / ROT_DIM)
    ang = pos.astype(F32)[:, None] * inv[None, :]
    d = jnp.arange(2 * HEAD_DIM) % HEAD_DIM
    f = d % HALF_ROT
    cos_f = jnp.cos(ang)[:, f]
    sin_f = jnp.sin(ang)[:, f]
    lo = (d < HALF_ROT)[None, :]
    hi = ((d >= HALF_ROT) & (d < ROT_DIM))[None, :]
    cos_t = jnp.where(lo | hi, cos_f, 1.0)
    s_next = jnp.where(lo, -sin_f, 0.0)
    s_prev = jnp.where(hi, sin_f, 0.0)
    return cos_t, s_next, s_prev


def _inproj_kernel(x_ref, gn_ref, w_ref, cos_ref, snext_ref, sprev_ref,
                   q_ref, k_ref, v_ref, sa_ref, g_ref, sc_ref, u_scr, ca_scr):
    n = pl.program_id(1)
    seg = w_ref.shape[1]

    def segment(epilogue):
        u = u_scr[...]
        for c0 in range(0, seg, INPROJ_CHUNK):
            cols = slice(c0, c0 + INPROJ_CHUNK)
            epilogue(jnp.dot(u, w_ref[:, cols], preferred_element_type=F32), cols)

    def rope_to(out_ref, scale):
        def epilogue(z, cols):
            cos_t, s_next, s_prev = cos_ref[...], snext_ref[...], sprev_ref[...]
            for h in range(z.shape[1] // LANES):
                zh = z[:, h * LANES:(h + 1) * LANES]
                nxt = pltpu.roll(zh, LANES - HALF_ROT, 1)
                prv = pltpu.roll(zh, HALF_ROT, 1)
                r = zh * cos_t + nxt * s_next + prv * s_prev
                if scale != 1.0:
                    r = r * scale
                out_ref[:, cols.start + h * LANES:cols.start + (h + 1) * LANES] = r.astype(out_ref.dtype)
        return epilogue

    def store_to(out_ref, fn):
        def epilogue(z, cols):
            out_ref[:, cols] = fn(z, cols).astype(out_ref.dtype)
        return epilogue

    @pl.when(n == 0)
    def _q():
        u_scr[...] = _rms(x_ref[...], gn_ref[...]).astype(BF16)
        segment(rope_to(q_ref, Q_SCALE))

    @pl.when(n == 1)
    def _k():
        segment(rope_to(k_ref, 1.0))

    @pl.when(n == 2)
    def _v():
        segment(store_to(v_ref, lambda z, cols: z))

    @pl.when(n == 3)
    def _gate_a():
        segment(store_to(sa_ref, lambda z, cols: _silu(z)))

    @pl.when(n == 4)
    def _glu_a():
        segment(store_to(ca_scr, lambda z, cols: z))

    @pl.when(n == 5)
    def _glu_b():
        segment(store_to(g_ref, lambda z, cols: ca_scr[:, cols] * _sigmoid(z)))

    @pl.when(n == 6)
    def _gate_c():
        segment(store_to(sc_ref, lambda z, cols: _silu(z)))


def _inproj(x2d, g_norm, w_in_bf, tables, tm, seg):
    m, d = x2d.shape
    n_seg = w_in_bf.shape[1] // seg
    assert n_seg == 7 and m % tm == 0
    cos_t, s_next, s_prev = tables
    t_blocks = cos_t.shape[0] // tm
    row = lambda i, n: (i, 0)
    tab = lambda i, n: (i % t_blocks, 0)
    out_sds = lambda dt: jax.ShapeDtypeStruct((m, seg), dt)
    return pl.pallas_call(
        _inproj_kernel,
        grid=(m // tm, n_seg),
        in_specs=[
            pl.BlockSpec((tm, d), row),
            pl.BlockSpec((1, d), lambda i, n: (0, 0)),
            pl.BlockSpec((d, seg), lambda i, n: (0, n)),
            pl.BlockSpec((tm, LANES), tab),
            pl.BlockSpec((tm, LANES), tab),
            pl.BlockSpec((tm, LANES), tab),
        ],
        out_specs=[pl.BlockSpec((tm, seg), row)] * 6,
        out_shape=[out_sds(BF16), out_sds(F32), out_sds(F32), out_sds(BF16), out_sds(F32), out_sds(BF16)],
        scratch_shapes=[pltpu.VMEM((tm, d), BF16), pltpu.VMEM((tm, seg), F32)],
        compiler_params=pltpu.CompilerParams(
            dimension_semantics=("parallel", "arbitrary"), vmem_limit_bytes=VMEM_LIMIT),
        name="inproj",
    )(x2d, g_norm.reshape(1, d), w_in_bf, cos_t, s_next, s_prev)


def _attn_kernel(pt_ref, lq1, lk1, lq2, lk2, gs_ref, q_ref, k_ref, v_ref, sa_ref,
                 qd_ref, knd_ref, vnd_ref, sad_ref, ck_hbm, cv_hbm, o_ref, od_ref,
                 kt_scr, vb_scr, kbuf, vbuf, sems, m_scr, l_scr, acca_scr, accb_scr,
                 *, tq, lam_init, layer, steps_per_seq):
    s_len, hw = k_ref.shape
    nh = qd_ref.shape[0]
    half = kbuf.shape[1]
    n_pages = pt_ref.shape[1]
    step = pl.program_id(0) * pl.num_programs(1) + pl.program_id(1)
    n_steps = pl.num_programs(0) * pl.num_programs(1)
    lam = _lam(lq1, lk1, lq2, lk2, lam_init)
    gs = gs_ref[...]

    def page_copies(step_idx, hb, i):
        g = (step_idx * N_HALVES + hb) * half + i
        pid = pt_ref[g // n_pages, g % n_pages]
        return (pltpu.make_async_copy(ck_hbm.at[layer, pid], kbuf.at[hb, i], sems.at[0, hb, i]),
                pltpu.make_async_copy(cv_hbm.at[layer, pid], vbuf.at[hb, i], sems.at[1, hb, i]))

    def start_half(step_idx, hb):
        for i in range(half):
            for copy in page_copies(step_idx, hb, i):
                copy.start()

    def wait_half(hb):
        for i in range(half):
            for copy in page_copies(step, hb, i):
                copy.wait()

    def refill_half(hb):
        @pl.when(step + 1 < n_steps)
        def _():
            start_half(step + 1, hb)

    @pl.when(step == 0)
    def _prime():
        for hb in range(N_HALVES):
            start_half(0, hb)

    @pl.when(step % steps_per_seq == 0)
    def _init():
        m_scr[...] = jnp.full(m_scr.shape, NEG_INF, F32)
        l_scr[...] = jnp.zeros(l_scr.shape, F32)
        acca_scr[...] = jnp.zeros(acca_scr.shape, F32)
        accb_scr[...] = jnp.zeros(accb_scr.shape, F32)

    qd = qd_ref[...].astype(F32)
    d_map = lax.broadcasted_iota(jnp.int32, (hw, hw), 0) // HEAD_DIM
    n_map = lax.broadcasted_iota(jnp.int32, (hw, hw), 1) // HEAD_DIM
    sel = (d_map == n_map).astype(BF16)

    def swap_maps(x):
        return pltpu.roll(x, HEAD_DIM, x.ndim - 1)

    def decode_update(t, kp, vp):
        rows = kp.shape[0]
        prod = (kp * qd[None]).reshape(rows * nh, hw)
        s = jnp.dot(prod.astype(BF16), sel, preferred_element_type=F32)
        m_old = m_scr[t]
        m_new = jnp.maximum(m_old, jnp.max(s.reshape(rows, nh, hw), axis=0))
        alpha = jnp.exp2(m_old - m_new)
        p2 = jnp.exp2(s.reshape(rows, nh, hw) - m_new[None]).reshape(rows * nh, hw)
        p = p2.reshape(rows, nh, hw)
        p_sw = swap_maps(p2).reshape(rows, nh, hw)
        l_scr[t] = alpha * l_scr[t] + jnp.sum(p, axis=0)
        acca_scr[t] = alpha * acca_scr[t] + jnp.sum(p * vp, axis=0)
        accb_scr[t] = swap_maps(alpha) * accb_scr[t] + jnp.sum(p_sw * vp, axis=0)
        m_scr[t] = m_new

    kt_scr[...] = k_ref[...].T.astype(BF16)
    vb_scr[:, 0:hw] = v_ref[...].astype(BF16)
    vb_scr[:, hw:2 * hw] = jnp.ones((s_len, hw), BF16)
    lane = lax.broadcasted_iota(jnp.int32, (tq, hw), 1)
    q_row = lax.broadcasted_iota(jnp.int32, (2 * tq, tq), 0) % tq
    k_col = lax.broadcasted_iota(jnp.int32, (2 * tq, tq), 1)
    causal = k_col <= q_row

    def attention_tiles(q_tiles):
        for i in q_tiles:
            r0 = i * tq
            q = q_ref[r0:r0 + tq, :]
            zero = jnp.zeros_like(q)
            q2 = jnp.concatenate(
                [jnp.where(lane < HEAD_DIM, q, zero), jnp.where(lane >= HEAD_DIM, q, zero)], axis=0)
            m = jnp.full((2 * tq, 1), NEG_INF, F32)
            acc = jnp.zeros((2 * tq, 2 * hw), F32)
            for j in range(i + 1):
                c0 = j * tq
                s = jnp.dot(q2, kt_scr[:, c0:c0 + tq], preferred_element_type=F32)
                if j == i:
                    s = jnp.where(causal, s, NEG_INF)
                m_new = jnp.maximum(m, jnp.max(s, axis=-1, keepdims=True))
                alpha = jnp.exp2(m - m_new)
                p = jnp.exp2(s - m_new).astype(BF16)
                acc = alpha * acc + jnp.dot(p, vb_scr[c0:c0 + tq, :], preferred_element_type=F32)
                m = m_new
                yield
            o = acc[0:tq, 0:hw] / acc[0:tq, hw:] - lam * (acc[tq:, 0:hw] / acc[tq:, hw:])
            o = _rms(o, gs) * (1.0 - lam_init)
            o_ref[r0:r0 + tq, :] = (o * sa_ref[r0:r0 + tq, :].astype(F32)).astype(o_ref.dtype)

    def half_step(q_tiles, hb):
        n_tiles = sum(i + 1 for i in q_tiles)
        pages_done = 0
        for tiles_done, _ in enumerate(attention_tiles(q_tiles), 1):
            while pages_done < tiles_done * half // n_tiles:
                decode_update(pages_done % N_SLOTS, kbuf[hb, pages_done], vbuf[hb, pages_done])
                pages_done += 1

    n_q = s_len // tq
    tiles_upto = lambda n: n * (n + 1) // 2
    split = min(range(n_q + 1), key=lambda n: abs(2 * tiles_upto(n) - tiles_upto(n_q)))
    for hb, q_tiles in enumerate((range(0, split), range(split, n_q))):
        wait_half(hb)
        half_step(q_tiles, hb)
        refill_half(hb)

    @pl.when(step % steps_per_seq == steps_per_seq - 1)
    def _finish():
        decode_update(0, knd_ref[...][None], vnd_ref[...][None])
        m_all = m_scr[...]
        m = jnp.max(m_all, axis=0)
        w = jnp.exp2(m_all - m[None])
        l = jnp.sum(w * l_scr[...], axis=0)
        acc_a = jnp.sum(w * acca_scr[...], axis=0)
        acc_b = jnp.sum(swap_maps(w.reshape(N_SLOTS * nh, hw)).reshape(N_SLOTS, nh, hw) * accb_scr[...], axis=0)
        na = acc_a / l
        nb = acc_b / swap_maps(l)
        lo = lax.broadcasted_iota(jnp.int32, (nh, hw), 1) < HEAD_DIM
        o = jnp.where(lo, na, nb) - lam * jnp.where(lo, nb, na)
        o = _rms(o, gs) * (1.0 - lam_init)
        od_ref[...] = o * sad_ref[...].astype(F32)


def _attention(q, k, v, sa, qd, knd, vnd, sad, cache_k, cache_v, page_table, layer, lams, g_subln, tq, lam_init):
    b, s, aw = q.shape
    db, nh, hw = qd.shape
    n_pages, page = page_table.shape[1], cache_k.shape[2]
    n_steps = b * nh
    assert aw == nh * hw and (db * n_pages) % (n_steps * N_HALVES) == 0
    pages_per_step = db * n_pages // n_steps
    assert n_pages % pages_per_step == 0
    steps_per_seq = n_pages // pages_per_step
    half = pages_per_step // N_HALVES
    head = lambda bi, h, pt: (bi, 0, h)
    small = lambda bi, h, pt: (0, 0)
    tok = lambda bi, h, pt: ((bi * nh + h) // steps_per_seq, 0, 0)
    head_spec = pl.BlockSpec((None, s, hw), head)
    tok_spec = pl.BlockSpec((None, nh, hw), tok)
    hbm_spec = pl.BlockSpec(memory_space=pl.ANY)
    grid_spec = pltpu.PrefetchScalarGridSpec(
        num_scalar_prefetch=1,
        grid=(b, nh),
        in_specs=[pl.BlockSpec((1, HEAD_DIM), small)] * 4 + [pl.BlockSpec((1, hw), small)]
        + [head_spec] * 4 + [tok_spec] * 4 + [hbm_spec] * 2,
        out_specs=[head_spec, tok_spec],
        scratch_shapes=[
            pltpu.VMEM((hw, s), BF16), pltpu.VMEM((s, 2 * hw), BF16),
            pltpu.VMEM((N_HALVES, half, page, nh, hw), F32), pltpu.VMEM((N_HALVES, half, page, nh, hw), F32),
            pltpu.SemaphoreType.DMA((2, N_HALVES, half)),
        ] + [pltpu.VMEM((N_SLOTS, nh, hw), F32)] * 4,
    )
    return pl.pallas_call(
        functools.partial(_attn_kernel, tq=tq, lam_init=lam_init, layer=layer, steps_per_seq=steps_per_seq),
        grid_spec=grid_spec,
        out_shape=[jax.ShapeDtypeStruct((b, s, aw), BF16), jax.ShapeDtypeStruct((db, nh, hw), F32)],
        compiler_params=pltpu.CompilerParams(
            dimension_semantics=("arbitrary", "arbitrary"), vmem_limit_bytes=VMEM_LIMIT),
        name="attention",
    )(page_table, *lams, g_subln.reshape(1, hw), q, k, v, sa, qd, knd, vnd, sad, cache_k, cache_v)


def _conv_kernel(gc_ref, gh_ref, wdw_ref, bdw_ref, gcl_ref, bcl_ref, wpw_ref, sc_ref, o_ref,
                 sh_scr, c_scr, *, ts, tc, halo, kw, rc):
    i = pl.program_id(1)
    c = pl.program_id(2)
    nc = c_scr.shape[0]
    off = halo - (kw - 1)
    n_copy = halo + ts - SUBLANES
    sh_scr[0, 0:halo, :] = jnp.where(i == 0, 0.0, gh_ref[...])
    sh_scr[0, halo:halo + ts, :] = gc_ref[...]
    for ph in range(1, SUBLANES):
        sh_scr[ph, 0:n_copy, :] = sh_scr[0, ph:ph + n_copy, :]
    for r in range(ts // rc):
        acc = jnp.broadcast_to(bdw_ref[...], (rc, tc))
        for j in range(kw):
            ph, base = (off + j) % SUBLANES, (off + j) // SUBLANES * SUBLANES
            acc = acc + wdw_ref[j:j + 1, :] * sh_scr[ph, r * rc + base:r * rc + base + rc, :]
        c_scr[c, r * rc:(r + 1) * rc, :] = acc

    @pl.when(c == nc - 1)
    def _norm_pw():
        width = nc * tc
        cs = [c_scr[k] for k in range(nc)]
        mu = sum(jnp.sum(x, axis=-1, keepdims=True) for x in cs) / width
        var = sum(jnp.sum(jnp.square(x - mu), axis=-1, keepdims=True) for x in cs) / width
        inv = lax.rsqrt(var + EPS)
        acc = jnp.zeros(o_ref.shape, F32)
        for k in range(nc):
            y = (cs[k] - mu) * inv * gcl_ref[:, k * tc:(k + 1) * tc] + bcl_ref[:, k * tc:(k + 1) * tc]
            acc = acc + jnp.dot(_silu(y).astype(BF16), wpw_ref[k * tc:(k + 1) * tc, :],
                                preferred_element_type=F32)
        o_ref[...] = (acc * sc_ref[...].astype(F32)).astype(o_ref.dtype)


def _prompt_conv(g, sc, w_dw, b_dw, g_cln, b_cln, w_pw_bf, ts, tc):
    b, s, cw = g.shape
    kw = w_dw.shape[0]
    halo = -(-(kw - 1) // SUBLANES) * SUBLANES
    assert ts % halo == 0 and s % ts == 0 and cw % tc == 0
    hb = ts // halo
    vec = lambda a: a.reshape(1, cw)
    full = lambda bi, i, c: (0, 0)
    return pl.pallas_call(
        functools.partial(_conv_kernel, ts=ts, tc=tc, halo=halo, kw=kw, rc=64),
        grid=(b, s // ts, cw // tc),
        in_specs=[
            pl.BlockSpec((None, ts, tc), lambda bi, i, c: (bi, i, c)),
            pl.BlockSpec((None, halo, tc), lambda bi, i, c: (bi, jnp.maximum(i * hb - 1, 0), c)),
            pl.BlockSpec((kw, tc), lambda bi, i, c: (0, c)),
            pl.BlockSpec((1, tc), lambda bi, i, c: (0, c)),
            pl.BlockSpec((1, cw), full),
            pl.BlockSpec((1, cw), full),
            pl.BlockSpec((cw, cw), full),
            pl.BlockSpec((None, ts, cw), lambda bi, i, c: (bi, i, 0)),
        ],
        out_specs=pl.BlockSpec((None, ts, cw), lambda bi, i, c: (bi, i, 0)),
        out_shape=jax.ShapeDtypeStruct((b, s, cw), BF16),
        scratch_shapes=[pltpu.VMEM((SUBLANES, halo + ts, tc), F32),
                        pltpu.VMEM((cw // tc, ts, tc), F32)],
        compiler_params=pltpu.CompilerParams(
            dimension_semantics=("parallel", "parallel", "arbitrary"), vmem_limit_bytes=VMEM_LIMIT),
        name="prompt_conv",
    )(g, g, w_dw, vec(b_dw), vec(g_cln), vec(b_cln), w_pw_bf, sc)


def _decode_conv_kernel(hist_ref, wdw_ref, bdw_ref, gcl_ref, bcl_ref, wpw_ref, sc_ref, o_ref):
    kw = hist_ref.shape[0]
    acc = jnp.broadcast_to(bdw_ref[...], o_ref.shape)
    for j in range(kw):
        acc = acc + wdw_ref[j:j + 1, :] * hist_ref[j]
    mu = jnp.mean(acc, axis=-1, keepdims=True)
    var = jnp.mean(jnp.square(acc - mu), axis=-1, keepdims=True)
    y = (acc - mu) * lax.rsqrt(var + EPS) * gcl_ref[...] + bcl_ref[...]
    c = jnp.dot(_silu(y).astype(BF16), wpw_ref[...], preferred_element_type=F32)
    o_ref[...] = (c * sc_ref[...].astype(F32)).astype(o_ref.dtype)


def _decode_conv(hist_t, sc, w_dw, b_dw, g_cln, b_cln, w_pw_bf):
    kw, db, cw = hist_t.shape
    vec = lambda a: a.reshape(1, cw)
    return pl.pallas_call(
        _decode_conv_kernel,
        out_shape=jax.ShapeDtypeStruct((db, cw), BF16),
        compiler_params=pltpu.CompilerParams(vmem_limit_bytes=VMEM_LIMIT),
        name="decode_conv",
    )(hist_t, w_dw, vec(b_dw), vec(g_cln), vec(b_cln), w_pw_bf, sc)


def _out_kernel(x_ref, oa_ref, oc_ref, p_ref, wout_ref, gple_ref, wpg_ref, wple_ref, gfin_ref, y_ref,
                *, final_norm):
    tm, aw = oa_ref.shape
    sub = min(tm, OUT_SUBTILE)
    for r0 in range(0, tm, sub):
        rows = slice(r0, r0 + sub)
        h = x_ref[rows, :]
        h = h + jnp.dot(oa_ref[rows, :], wout_ref[0:aw, :], preferred_element_type=F32)
        h = h + jnp.dot(oc_ref[rows, :], wout_ref[aw:, :], preferred_element_type=F32)
        r = _rms(h, gple_ref[...]).astype(BF16)
        gate = _sigmoid(jnp.dot(r, wpg_ref[...], preferred_element_type=F32))
        e = jnp.dot(p_ref[rows, :].astype(BF16), wple_ref[...], preferred_element_type=F32)
        h = h + gate * e
        y_ref[rows, :] = _rms(h, gfin_ref[...]) if final_norm else h


def _outproj(x2d, oa, oc, p2d, w_out_bf, g_ple, w_pg_bf, w_ple_bf, g_final, tm, final_norm):
    m, d = x2d.shape
    aw, cw, pd = oa.shape[1], oc.shape[1], p2d.shape[1]
    row = lambda i: (i, 0)
    full = lambda i: (0, 0)
    once = pl.Buffered(1)
    return pl.pallas_call(
        functools.partial(_out_kernel, final_norm=final_norm),
        grid=(m // tm,),
        in_specs=[
            pl.BlockSpec((tm, d), row),
            pl.BlockSpec((tm, aw), row),
            pl.BlockSpec((tm, cw), row),
            pl.BlockSpec((tm, pd), row),
            pl.BlockSpec((aw + cw, d), full, pipeline_mode=once),
            pl.BlockSpec((1, d), full),
            pl.BlockSpec((d, d), full, pipeline_mode=once),
            pl.BlockSpec((pd, d), full, pipeline_mode=once),
            pl.BlockSpec((1, d), full),
        ],
        out_specs=pl.BlockSpec((tm, d), row),
        out_shape=jax.ShapeDtypeStruct((m, d), F32),
        compiler_params=pltpu.CompilerParams(
            dimension_semantics=("parallel",), vmem_limit_bytes=VMEM_LIMIT),
        name="outproj",
    )(x2d, oa, oc, p2d, w_out_bf, g_ple.reshape(1, d), w_pg_bf, w_ple_bf, g_final.reshape(1, d))


def kernel(x_prompt, x_sample, p_prompt, p_sample, cache_k, cache_v, state_conv, page_table, w_in, g_norm, lam_q1, lam_k1, lam_q2, lam_k2, g_subln, w_dw, b_dw, g_cln, b_cln, w_pw, w_out, g_ple, w_pg, w_ple, g_final):
    b, s, d = x_prompt.shape
    db, ds, _ = x_sample.shape
    depth = w_in.shape[0]
    nh, hw = cache_k.shape[3], cache_k.shape[4]
    aw = nh * hw
    cw = w_dw.shape[2]
    kw = w_dw.shape[1]
    past = page_table.shape[1] * cache_k.shape[2]
    assert ds == 1 and aw == cw and hw == 2 * HEAD_DIM

    tab_p = _rope_tables(jnp.arange(s, dtype=jnp.int32))
    tab_s = _rope_tables(jnp.full((db * ds,), past, dtype=jnp.int32))

    h_p = x_prompt.reshape(b * s, d)
    h_s = x_sample.reshape(db * ds, d)
    outs = {k: [] for k in ("kp", "vp", "cp", "ks", "vs", "cs")}
    for i in range(depth):
        lam_init = _lam_init(i)
        last = i == depth - 1
        lams = tuple(a[i].reshape(1, HEAD_DIM) for a in (lam_q1, lam_k1, lam_q2, lam_k2))
        w_in_bf, w_pw_bf = w_in[i].astype(BF16), w_pw[i].astype(BF16)
        w_out_bf, w_pg_bf, w_ple_bf = w_out[i].astype(BF16), w_pg[i].astype(BF16), w_ple[i].astype(BF16)

        qd, kd, vd, sad, gd, scd = _inproj(h_s, g_norm[i], w_in_bf, tab_s, tm=db * ds, seg=aw)
        q, k, v, sa, g, sc = _inproj(h_p, g_norm[i], w_in_bf, tab_p, tm=512, seg=aw)
        as3 = lambda a: a.reshape(b, s, -1)
        hd3 = lambda a: a.reshape(db, nh, hw)
        oa, oad = _attention(as3(q), as3(k), as3(v), as3(sa), hd3(qd), hd3(kd), hd3(vd), hd3(sad),
                             cache_k, cache_v, page_table, i, lams, g_subln[i], tq=256, lam_init=lam_init)

        oc = _prompt_conv(as3(g), as3(sc), w_dw[i], b_dw[i], g_cln[i], b_cln[i], w_pw_bf, ts=256, tc=256)
        h_p = _outproj(h_p, oa.reshape(b * s, aw), oc.reshape(b * s, cw), p_prompt[i].reshape(b * s, -1),
                       w_out_bf, g_ple[i], w_pg_bf, w_ple_bf, g_final, tm=512, final_norm=last)
        outs["kp"].append(k.reshape(b, s, nh, hw))
        outs["vp"].append(v.reshape(b, s, nh, hw))
        outs["cp"].append(as3(g)[:, s - (kw - 1):])

        hist = jnp.concatenate([state_conv[i], gd.reshape(db, ds, cw)], axis=1)
        ocd = _decode_conv(hist.transpose(1, 0, 2), scd, w_dw[i], b_dw[i], g_cln[i], b_cln[i], w_pw_bf)
        h_s = _outproj(h_s, oad.reshape(db, aw).astype(BF16), ocd, p_sample[i].reshape(db * ds, -1),
                       w_out_bf, g_ple[i], w_pg_bf, w_ple_bf, g_final, tm=db * ds, final_norm=last)
        outs["ks"].append(kd.reshape(db, ds, nh, hw))
        outs["vs"].append(vd.reshape(db, ds, nh, hw))
        outs["cs"].append(hist[:, 1:])

    st = lambda key: jnp.stack(outs[key])
    return (h_p.reshape(b, s, d), h_s.reshape(db, ds, d),
            st("kp"), st("vp"), st("cp"), st("ks"), st("vs"), st("cs"))
```

```python
import functools
import math

import jax
import jax.numpy as jnp
from jax import lax
from jax.experimental import pallas as pl
from jax.experimental.pallas import tpu as pltpu

F32 = jnp.float32
BF16 = jnp.bfloat16

HEAD_DIM = 64
ROT_DIM = HEAD_DIM // 4
HALF_ROT = ROT_DIM // 2
ROPE_THETA = 500000.0
EPS = 1e-6
NEG_INF = -1e30
Q_SCALE = HEAD_DIM ** -0.5 * math.log2(math.e)

LANES = 128
SUBLANES = 8
VMEM_LIMIT = 56 * 1024 * 1024
MXU_DIM = 256
INPROJ_CHUNK = MXU_DIM
OUT_SUBTILE = 256
N_SLOTS = 8
N_HALVES = 2


def _sigmoid(x):
    return 1.0 / (1.0 + jnp.exp(-x))


def _silu(x):
    return x * _sigmoid(x)


def _rms(x, g):
    return x * lax.rsqrt(jnp.mean(x * x, axis=-1, keepdims=True) + EPS) * g


def _lam_init(layer_idx):
    return 0.8 - 0.6 * math.exp(-0.3 * layer_idx)


def _lam(lq1, lk1, lq2, lk2, lam_init):
    a = jnp.sum(lq1[...] * lk1[...], keepdims=True)
    b = jnp.sum(lq2[...] * lk2[...], keepdims=True)
    return jnp.exp(a) - jnp.exp(b) + lam_init


def _rope_tables(pos):
    inv = ROPE_THETA ** (-jnp.arange(0, ROT_DIM, 2, dtype=F32) / ROT_DIM)
    ang = pos.astype(F32)[:, None] * inv[None, :]
    d = jnp.arange(2 * HEAD_DIM) % HEAD_DIM
    f = d % HALF_ROT
    cos_f = jnp.cos(ang)[:, f]
    sin_f = jnp.sin(ang)[:, f]
    lo = (d < HALF_ROT)[None, :]
    hi = ((d >= HALF_ROT) & (d < ROT_DIM))[None, :]
    cos_t = jnp.where(lo | hi, cos_f, 1.0)
    s_next = jnp.where(lo, -sin_f, 0.0)
    s_prev = jnp.where(hi, sin_f, 0.0)
    return cos_t, s_next, s_prev


def _inproj_kernel(x_ref, gn_ref, w_ref, cos_ref, snext_ref, sprev_ref,
                   q_ref, k_ref, v_ref, sa_ref, g_ref, sc_ref, u_scr, ca_scr):
    n = pl.program_id(1)
    seg = w_ref.shape[1]

    def segment(epilogue):
        u = u_scr[...]
        for c0 in range(0, seg, INPROJ_CHUNK):
            cols = slice(c0, c0 + INPROJ_CHUNK)
            epilogue(jnp.dot(u, w_ref[:, cols], preferred_element_type=F32), cols)

    def rope_to(out_ref, scale):
        def epilogue(z, cols):
            cos_t, s_next, s_prev = cos_ref[...], snext_ref[...], sprev_ref[...]
            for h in range(z.shape[1] // LANES):
                zh = z[:, h * LANES:(h + 1) * LANES]
                nxt = pltpu.roll(zh, LANES - HALF_ROT, 1)
                prv = pltpu.roll(zh, HALF_ROT, 1)
                r = zh * cos_t + nxt * s_next + prv * s_prev
                if scale != 1.0:
                    r = r * scale
                out_ref[:, cols.start + h * LANES:cols.start + (h + 1) * LANES] = r.astype(out_ref.dtype)
        return epilogue

    def store_to(out_ref, fn):
        def epilogue(z, cols):
            out_ref[:, cols] = fn(z, cols).astype(out_ref.dtype)
        return epilogue

    @pl.when(n == 0)
    def _q():
        u_scr[...] = _rms(x_ref[...], gn_ref[...]).astype(BF16)
        segment(rope_to(q_ref, Q_SCALE))

    @pl.when(n == 1)
    def _k():
        segment(rope_to(k_ref, 1.0))

    @pl.when(n == 2)
    def _v():
        segment(store_to(v_ref, lambda z, cols: z))

    @pl.when(n == 3)
    def _gate_a():
        segment(store_to(sa_ref, lambda z, cols: _silu(z)))

    @pl.when(n == 4)
    def _glu_a():
        segment(store_to(ca_scr, lambda z, cols: z))

    @pl.when(n == 5)
    def _glu_b():
        segment(store_to(g_ref, lambda z, cols: ca_scr[:, cols] * _sigmoid(z)))

    @pl.when(n == 6)
    def _gate_c():
        segment(store_to(sc_ref, lambda z, cols: _silu(z)))


def _inproj(x2d, g_norm, w_in_bf, tables, tm, seg):
    m, d = x2d.shape
    n_seg = w_in_bf.shape[1] // seg
    assert n_seg == 7 and m % tm == 0
    cos_t, s_next, s_prev = tables
    t_blocks = cos_t.shape[0] // tm
    row = lambda i, n: (i, 0)
    tab = lambda i, n: (i % t_blocks, 0)
    out_sds = lambda dt: jax.ShapeDtypeStruct((m, seg), dt)
    return pl.pallas_call(
        _inproj_kernel,
        grid=(m // tm, n_seg),
        in_specs=[
            pl.BlockSpec((tm, d), row),
            pl.BlockSpec((1, d), lambda i, n: (0, 0)),
            pl.BlockSpec((d, seg), lambda i, n: (0, n)),
            pl.BlockSpec((tm, LANES), tab),
            pl.BlockSpec((tm, LANES), tab),
            pl.BlockSpec((tm, LANES), tab),
        ],
        out_specs=[pl.BlockSpec((tm, seg), row)] * 6,
        out_shape=[out_sds(BF16), out_sds(F32), out_sds(F32), out_sds(BF16), out_sds(F32), out_sds(BF16)],
        scratch_shapes=[pltpu.VMEM((tm, d), BF16), pltpu.VMEM((tm, seg), F32)],
        compiler_params=pltpu.CompilerParams(
            dimension_semantics=("parallel", "arbitrary"), vmem_limit_bytes=VMEM_LIMIT),
        name="inproj",
    )(x2d, g_norm.reshape(1, d), w_in_bf, cos_t, s_next, s_prev)


def _attn_kernel(pt_ref, lq1, lk1, lq2, lk2, gs_ref, q_ref, k_ref, v_ref, sa_ref,
                 qd_ref, knd_ref, vnd_ref, sad_ref, ck_hbm, cv_hbm, o_ref, od_ref,
                 kt_scr, vb_scr, kbuf, vbuf, sems, m_scr, l_scr, acca_scr, accb_scr,
                 *, tq, lam_init, layer, steps_per_seq):
    s_len, hw = k_ref.shape
    nh = qd_ref.shape[0]
    half = kbuf.shape[1]
    n_pages = pt_ref.shape[1]
    step = pl.program_id(0) * pl.num_programs(1) + pl.program_id(1)
    n_steps = pl.num_programs(0) * pl.num_programs(1)
    lam = _lam(lq1, lk1, lq2, lk2, lam_init)
    gs = gs_ref[...]

    def page_copies(step_idx, hb, i):
        g = (step_idx * N_HALVES + hb) * half + i
        pid = pt_ref[g // n_pages, g % n_pages]
        return (pltpu.make_async_copy(ck_hbm.at[layer, pid], kbuf.at[hb, i], sems.at[0, hb, i]),
                pltpu.make_async_copy(cv_hbm.at[layer, pid], vbuf.at[hb, i], sems.at[1, hb, i]))

    def start_half(step_idx, hb):
        for i in range(half):
            for copy in page_copies(step_idx, hb, i):
                copy.start()

    def wait_half(hb):
        for i in range(half):
            for copy in page_copies(step, hb, i):
                copy.wait()

    def refill_half(hb):
        @pl.when(step + 1 < n_steps)
        def _():
            start_half(step + 1, hb)

    @pl.when(step == 0)
    def _prime():
        for hb in range(N_HALVES):
            start_half(0, hb)

    @pl.when(step % steps_per_seq == 0)
    def _init():
        m_scr[...] = jnp.full(m_scr.shape, NEG_INF, F32)
        l_scr[...] = jnp.zeros(l_scr.shape, F32)
        acca_scr[...] = jnp.zeros(acca_scr.shape, F32)
        accb_scr[...] = jnp.zeros(accb_scr.shape, F32)

    qd = qd_ref[...].astype(F32)
    d_map = lax.broadcasted_iota(jnp.int32, (hw, hw), 0) // HEAD_DIM
    n_map = lax.broadcasted_iota(jnp.int32, (hw, hw), 1) // HEAD_DIM
    sel = (d_map == n_map).astype(BF16)

    def swap_maps(x):
        return pltpu.roll(x, HEAD_DIM, x.ndim - 1)

    def decode_update(t, kp, vp):
        rows = kp.shape[0]
        prod = (kp * qd[None]).reshape(rows * nh, hw)
        s = jnp.dot(prod.astype(BF16), sel, preferred_element_type=F32)
        m_old = m_scr[t]
        m_new = jnp.maximum(m_old, jnp.max(s.reshape(rows, nh, hw), axis=0))
        alpha = jnp.exp2(m_old - m_new)
        p2 = jnp.exp2(s.reshape(rows, nh, hw) - m_new[None]).reshape(rows * nh, hw)
        p = p2.reshape(rows, nh, hw)
        p_sw = swap_maps(p2).reshape(rows, nh, hw)
        l_scr[t] = alpha * l_scr[t] + jnp.sum(p, axis=0)
        acca_scr[t] = alpha * acca_scr[t] + jnp.sum(p * vp, axis=0)
        accb_scr[t] = swap_maps(alpha) * accb_scr[t] + jnp.sum(p_sw * vp, axis=0)
        m_scr[t] = m_new

    kt_scr[...] = k_ref[...].T.astype(BF16)
    vb_scr[:, 0:hw] = v_ref[...].astype(BF16)
    vb_scr[:, hw:2 * hw] = jnp.ones((s_len, hw), BF16)
    lane = lax.broadcasted_iota(jnp.int32, (tq, hw), 1)
    q_row = lax.broadcasted_iota(jnp.int32, (2 * tq, tq), 0) % tq
    k_col = lax.broadcasted_iota(jnp.int32, (2 * tq, tq), 1)
    causal = k_col <= q_row

    def attention_tile(i):
        r0, c_diag = i * tq, i * tq
        q = q_ref[r0:r0 + tq, :]
        zero = jnp.zeros_like(q)
        q2 = jnp.concatenate(
            [jnp.where(lane < HEAD_DIM, q, zero), jnp.where(lane >= HEAD_DIM, q, zero)], axis=0)
        s_diag = jnp.dot(q2, kt_scr[:, c_diag:c_diag + tq], preferred_element_type=F32)
        s_diag = jnp.where(causal, s_diag, NEG_INF)
        m = jnp.max(s_diag, axis=-1, keepdims=True)
        if i > 0:
            s_past = jnp.dot(q2, kt_scr[:, 0:c_diag], preferred_element_type=F32)
            m = jnp.maximum(m, jnp.max(s_past, axis=-1, keepdims=True))
        acc = jnp.dot(jnp.exp2(s_diag - m).astype(BF16), vb_scr[c_diag:c_diag + tq, :],
                      preferred_element_type=F32)
        if i > 0:
            acc = acc + jnp.dot(jnp.exp2(s_past - m).astype(BF16), vb_scr[0:c_diag, :],
                                preferred_element_type=F32)
        o = acc[0:tq, 0:hw] / acc[0:tq, hw:] - lam * (acc[tq:, 0:hw] / acc[tq:, hw:])
        o = _rms(o, gs) * (1.0 - lam_init)
        o_ref[r0:r0 + tq, :] = (o * sa_ref[r0:r0 + tq, :].astype(F32)).astype(o_ref.dtype)

    def half_step(q_tiles, hb):
        work = sum(i + 1 for i in q_tiles)
        work_done = pages_done = 0
        for i in q_tiles:
            attention_tile(i)
            work_done += i + 1
            while pages_done < work_done * half // work:
                decode_update(pages_done % N_SLOTS, kbuf[hb, pages_done], vbuf[hb, pages_done])
                pages_done += 1

    n_q = s_len // tq
    tiles_upto = lambda n: n * (n + 1) // 2
    split = min(range(n_q + 1), key=lambda n: abs(2 * tiles_upto(n) - tiles_upto(n_q)))
    for hb, q_tiles in enumerate((range(0, split), range(split, n_q))):
        wait_half(hb)
        half_step(q_tiles, hb)
        refill_half(hb)

    @pl.when(step % steps_per_seq == steps_per_seq - 1)
    def _finish():
        decode_update(0, knd_ref[...][None], vnd_ref[...][None])
        m_all = m_scr[...]
        m = jnp.max(m_all, axis=0)
        w = jnp.exp2(m_all - m[None])
        l = jnp.sum(w * l_scr[...], axis=0)
        acc_a = jnp.sum(w * acca_scr[...], axis=0)
        acc_b = jnp.sum(swap_maps(w.reshape(N_SLOTS * nh, hw)).reshape(N_SLOTS, nh, hw) * accb_scr[...], axis=0)
        na = acc_a / l
        nb = acc_b / swap_maps(l)
        lo = lax.broadcasted_iota(jnp.int32, (nh, hw), 1) < HEAD_DIM
        o = jnp.where(lo, na, nb) - lam * jnp.where(lo, nb, na)
        o = _rms(o, gs) * (1.0 - lam_init)
        od_ref[...] = o * sad_ref[...].astype(F32)


def _attention(q, k, v, sa, qd, knd, vnd, sad, cache_k, cache_v, page_table, layer, lams, g_subln, tq, lam_init):
    b, s, aw = q.shape
    db, nh, hw = qd.shape
    n_pages, page = page_table.shape[1], cache_k.shape[2]
    n_steps = b * nh
    assert aw == nh * hw and (db * n_pages) % (n_steps * N_HALVES) == 0
    pages_per_step = db * n_pages // n_steps
    assert n_pages % pages_per_step == 0
    steps_per_seq = n_pages // pages_per_step
    half = pages_per_step // N_HALVES
    head = lambda bi, h, pt: (bi, 0, h)
    small = lambda bi, h, pt: (0, 0)
    tok = lambda bi, h, pt: ((bi * nh + h) // steps_per_seq, 0, 0)
    head_spec = pl.BlockSpec((None, s, hw), head)
    tok_spec = pl.BlockSpec((None, nh, hw), tok)
    hbm_spec = pl.BlockSpec(memory_space=pl.ANY)
    grid_spec = pltpu.PrefetchScalarGridSpec(
        num_scalar_prefetch=1,
        grid=(b, nh),
        in_specs=[pl.BlockSpec((1, HEAD_DIM), small)] * 4 + [pl.BlockSpec((1, hw), small)]
        + [head_spec] * 4 + [tok_spec] * 4 + [hbm_spec] * 2,
        out_specs=[head_spec, tok_spec],
        scratch_shapes=[
            pltpu.VMEM((hw, s), BF16), pltpu.VMEM((s, 2 * hw), BF16),
            pltpu.VMEM((N_HALVES, half, page, nh, hw), F32), pltpu.VMEM((N_HALVES, half, page, nh, hw), F32),
            pltpu.SemaphoreType.DMA((2, N_HALVES, half)),
        ] + [pltpu.VMEM((N_SLOTS, nh, hw), F32)] * 4,
    )
    return pl.pallas_call(
        functools.partial(_attn_kernel, tq=tq, lam_init=lam_init, layer=layer, steps_per_seq=steps_per_seq),
        grid_spec=grid_spec,
        out_shape=[jax.ShapeDtypeStruct((b, s, aw), BF16), jax.ShapeDtypeStruct((db, nh, hw), F32)],
        compiler_params=pltpu.CompilerParams(
            dimension_semantics=("arbitrary", "arbitrary"), vmem_limit_bytes=VMEM_LIMIT),
        name="attention",
    )(page_table, *lams, g_subln.reshape(1, hw), q, k, v, sa, qd, knd, vnd, sad, cache_k, cache_v)


def _conv_kernel(gc_ref, gh_ref, wdw_ref, bdw_ref, gcl_ref, bcl_ref, wpw_ref, sc_ref, o_ref,
                 sh_scr, c_scr, *, ts, tc, halo, kw, rc):
    i = pl.program_id(1)
    c = pl.program_id(2)
    nc = c_scr.shape[0]
    off = halo - (kw - 1)
    n_copy = halo + ts - SUBLANES
    sh_scr[0, 0:halo, :] = jnp.where(i == 0, 0.0, gh_ref[...])
    sh_scr[0, halo:halo + ts, :] = gc_ref[...]
    for ph in range(1, SUBLANES):
        sh_scr[ph, 0:n_copy, :] = sh_scr[0, ph:ph + n_copy, :]
    for r in range(ts // rc):
        acc = jnp.broadcast_to(bdw_ref[...], (rc, tc))
        for j in range(kw):
            ph, base = (off + j) % SUBLANES, (off + j) // SUBLANES * SUBLANES
            acc = acc + wdw_ref[j:j + 1, :] * sh_scr[ph, r * rc + base:r * rc + base + rc, :]
        c_scr[c, r * rc:(r + 1) * rc, :] = acc

    @pl.when(c == nc - 1)
    def _norm_pw():
        width = nc * tc
        cs = [c_scr[k] for k in range(nc)]
        mu = sum(jnp.sum(x, axis=-1, keepdims=True) for x in cs) / width
        var = sum(jnp.sum(jnp.square(x - mu), axis=-1, keepdims=True) for x in cs) / width
        inv = lax.rsqrt(var + EPS)
        acc = jnp.zeros(o_ref.shape, F32)
        for k in range(nc):
            y = (cs[k] - mu) * inv * gcl_ref[:, k * tc:(k + 1) * tc] + bcl_ref[:, k * tc:(k + 1) * tc]
            acc = acc + jnp.dot(_silu(y).astype(BF16), wpw_ref[k * tc:(k + 1) * tc, :],
                                preferred_element_type=F32)
        o_ref[...] = (acc * sc_ref[...].astype(F32)).astype(o_ref.dtype)


def _prompt_conv(g, sc, w_dw, b_dw, g_cln, b_cln, w_pw_bf, ts, tc):
    b, s, cw = g.shape
    kw = w_dw.shape[0]
    halo = -(-(kw - 1) // SUBLANES) * SUBLANES
    assert ts % halo == 0 and s % ts == 0 and cw % tc == 0
    hb = ts // halo
    vec = lambda a: a.reshape(1, cw)
    full = lambda bi, i, c: (0, 0)
    return pl.pallas_call(
        functools.partial(_conv_kernel, ts=ts, tc=tc, halo=halo, kw=kw, rc=64),
        grid=(b, s // ts, cw // tc),
        in_specs=[
            pl.BlockSpec((None, ts, tc), lambda bi, i, c: (bi, i, c)),
            pl.BlockSpec((None, halo, tc), lambda bi, i, c: (bi, jnp.maximum(i * hb - 1, 0), c)),
            pl.BlockSpec((kw, tc), lambda bi, i, c: (0, c)),
            pl.BlockSpec((1, tc), lambda bi, i, c: (0, c)),
            pl.BlockSpec((1, cw), full),
            pl.BlockSpec((1, cw), full),
            pl.BlockSpec((cw, cw), full),
            pl.BlockSpec((None, ts, cw), lambda bi, i, c: (bi, i, 0)),
        ],
        out_specs=pl.BlockSpec((None, ts, cw), lambda bi, i, c: (bi, i, 0)),
        out_shape=jax.ShapeDtypeStruct((b, s, cw), BF16),
        scratch_shapes=[pltpu.VMEM((SUBLANES, halo + ts, tc), F32),
                        pltpu.VMEM((cw // tc, ts, tc), F32)],
        compiler_params=pltpu.CompilerParams(
            dimension_semantics=("parallel", "parallel", "arbitrary"), vmem_limit_bytes=VMEM_LIMIT),
        name="prompt_conv",
    )(g, g, w_dw, vec(b_dw), vec(g_cln), vec(b_cln), w_pw_bf, sc)


def _decode_conv_kernel(hist_ref, wdw_ref, bdw_ref, gcl_ref, bcl_ref, wpw_ref, sc_ref, o_ref):
    kw = hist_ref.shape[0]
    acc = jnp.broadcast_to(bdw_ref[...], o_ref.shape)
    for j in range(kw):
        acc = acc + wdw_ref[j:j + 1, :] * hist_ref[j]
    mu = jnp.mean(acc, axis=-1, keepdims=True)
    var = jnp.mean(jnp.square(acc - mu), axis=-1, keepdims=True)
    y = (acc - mu) * lax.rsqrt(var + EPS) * gcl_ref[...] + bcl_ref[...]
    c = jnp.dot(_silu(y).astype(BF16), wpw_ref[...], preferred_element_type=F32)
    o_ref[...] = (c * sc_ref[...].astype(F32)).astype(o_ref.dtype)


def _decode_conv(hist_t, sc, w_dw, b_dw, g_cln, b_cln, w_pw_bf):
    kw, db, cw = hist_t.shape
    vec = lambda a: a.reshape(1, cw)
    return pl.pallas_call(
        _decode_conv_kernel,
        out_shape=jax.ShapeDtypeStruct((db, cw), BF16),
        compiler_params=pltpu.CompilerParams(vmem_limit_bytes=VMEM_LIMIT),
        name="decode_conv",
    )(hist_t, w_dw, vec(b_dw), vec(g_cln), vec(b_cln), w_pw_bf, sc)


def _out_kernel(x_ref, oa_ref, oc_ref, p_ref, wout_ref, gple_ref, wpg_ref, wple_ref, gfin_ref, y_ref,
                *, final_norm):
    tm, aw = oa_ref.shape
    sub = min(tm, OUT_SUBTILE)
    for r0 in range(0, tm, sub):
        rows = slice(r0, r0 + sub)
        h = x_ref[rows, :]
        h = h + jnp.dot(oa_ref[rows, :], wout_ref[0:aw, :], preferred_element_type=F32)
        h = h + jnp.dot(oc_ref[rows, :], wout_ref[aw:, :], preferred_element_type=F32)
        r = _rms(h, gple_ref[...]).astype(BF16)
        gate = _sigmoid(jnp.dot(r, wpg_ref[...], preferred_element_type=F32))
        e = jnp.dot(p_ref[rows, :].astype(BF16), wple_ref[...], preferred_element_type=F32)
        h = h + gate * e
        y_ref[rows, :] = _rms(h, gfin_ref[...]) if final_norm else h


def _outproj(x2d, oa, oc, p2d, w_out_bf, g_ple, w_pg_bf, w_ple_bf, g_final, tm, final_norm):
    m, d = x2d.shape
    aw, cw, pd = oa.shape[1], oc.shape[1], p2d.shape[1]
    row = lambda i: (i, 0)
    full = lambda i: (0, 0)
    once = pl.Buffered(1)
    return pl.pallas_call(
        functools.partial(_out_kernel, final_norm=final_norm),
        grid=(m // tm,),
        in_specs=[
            pl.BlockSpec((tm, d), row),
            pl.BlockSpec((tm, aw), row),
            pl.BlockSpec((tm, cw), row),
            pl.BlockSpec((tm, pd), row),
            pl.BlockSpec((aw + cw, d), full, pipeline_mode=once),
            pl.BlockSpec((1, d), full),
            pl.BlockSpec((d, d), full, pipeline_mode=once),
            pl.BlockSpec((pd, d), full, pipeline_mode=once),
            pl.BlockSpec((1, d), full),
        ],
        out_specs=pl.BlockSpec((tm, d), row),
        out_shape=jax.ShapeDtypeStruct((m, d), F32),
        compiler_params=pltpu.CompilerParams(
            dimension_semantics=("parallel",), vmem_limit_bytes=VMEM_LIMIT),
        name="outproj",
    )(x2d, oa, oc, p2d, w_out_bf, g_ple.reshape(1, d), w_pg_bf, w_ple_bf, g_final.reshape(1, d))


def kernel(x_prompt, x_sample, p_prompt, p_sample, cache_k, cache_v, state_conv, page_table, w_in, g_norm, lam_q1, lam_k1, lam_q2, lam_k2, g_subln, w_dw, b_dw, g_cln, b_cln, w_pw, w_out, g_ple, w_pg, w_ple, g_final):
    b, s, d = x_prompt.shape
    db, ds, _ = x_sample.shape
    depth = w_in.shape[0]
    nh, hw = cache_k.shape[3], cache_k.shape[4]
    aw = nh * hw
    cw = w_dw.shape[2]
    kw = w_dw.shape[1]
    past = page_table.shape[1] * cache_k.shape[2]
    assert ds == 1 and aw == cw and hw == 2 * HEAD_DIM

    tab_p = _rope_tables(jnp.arange(s, dtype=jnp.int32))
    tab_s = _rope_tables(jnp.full((db * ds,), past, dtype=jnp.int32))

    h_p = x_prompt.reshape(b * s, d)
    h_s = x_sample.reshape(db * ds, d)
    outs = {k: [] for k in ("kp", "vp", "cp", "ks", "vs", "cs")}
    for i in range(depth):
        lam_init = _lam_init(i)
        last = i == depth - 1
        lams = tuple(a[i].reshape(1, HEAD_DIM) for a in (lam_q1, lam_k1, lam_q2, lam_k2))
        w_in_bf, w_pw_bf = w_in[i].astype(BF16), w_pw[i].astype(BF16)
        w_out_bf, w_pg_bf, w_ple_bf = w_out[i].astype(BF16), w_pg[i].astype(BF16), w_ple[i].astype(BF16)

        qd, kd, vd, sad, gd, scd = _inproj(h_s, g_norm[i], w_in_bf, tab_s, tm=db * ds, seg=aw)
        q, k, v, sa, g, sc = _inproj(h_p, g_norm[i], w_in_bf, tab_p, tm=512, seg=aw)
        as3 = lambda a: a.reshape(b, s, -1)
        hd3 = lambda a: a.reshape(db, nh, hw)
        oa, oad = _attention(as3(q), as3(k), as3(v), as3(sa), hd3(qd), hd3(kd), hd3(vd), hd3(sad),
                             cache_k, cache_v, page_table, i, lams, g_subln[i], tq=256, lam_init=lam_init)

        oc = _prompt_conv(as3(g), as3(sc), w_dw[i], b_dw[i], g_cln[i], b_cln[i], w_pw_bf, ts=256, tc=256)
        h_p = _outproj(h_p, oa.reshape(b * s, aw), oc.reshape(b * s, cw), p_prompt[i].reshape(b * s, -1),
                       w_out_bf, g_ple[i], w_pg_bf, w_ple_bf, g_final, tm=512, final_norm=last)
        outs["kp"].append(k.reshape(b, s, nh, hw))
        outs["vp"].append(v.reshape(b, s, nh, hw))
        outs["cp"].append(as3(g)[:, s - (kw - 1):])

        hist = jnp.concatenate([state_conv[i], gd.reshape(db, ds, cw)], axis=1)
        ocd = _decode_conv(hist.transpose(1, 0, 2), scd, w_dw[i], b_dw[i], g_cln[i], b_cln[i], w_pw_bf)
        h_s = _outproj(h_s, oad.reshape(db, aw).astype(BF16), ocd, p_sample[i].reshape(db * ds, -1),
                       w_out_bf, g_ple[i], w_pg_bf, w_ple_bf, g_final, tm=db * ds, final_norm=last)
        outs["ks"].append(kd.reshape(db, ds, nh, hw))
        outs["vs"].append(vd.reshape(db, ds, nh, hw))
        outs["cs"].append(hist[:, 1:])

    st = lambda key: jnp.stack(outs[key])
    return (h_p.reshape(b, s, d), h_s.reshape(db, ds, d),
            st("kp"), st("vp"), st("cp"), st("ks"), st("vs"), st("cs"))
```

```python
import functools
import math

import jax
import jax.numpy as jnp
from jax import lax
from jax.experimental import pallas as pl
from jax.experimental.pallas import tpu as pltpu

F32 = jnp.float32
BF16 = jnp.bfloat16

HEAD_DIM = 64
ROT_DIM = HEAD_DIM // 4
HALF_ROT = ROT_DIM // 2
ROPE_THETA = 500000.0
EPS = 1e-6
NEG_INF = -1e30
Q_SCALE = HEAD_DIM ** -0.5 * math.log2(math.e)

LANES = 128
SUBLANES = 8
VMEM_LIMIT = 56 * 1024 * 1024
MXU_DIM = 256
INPROJ_CHUNK = MXU_DIM
OUT_SUBTILE = 256
N_HALVES = 2
DECODE_REDUCE_WAYS = 4


def _sigmoid(x):
    return 1.0 / (1.0 + jnp.exp(-x))


def _silu(x):
    return x * _sigmoid(x)


def _rms(x, g):
    return x * lax.rsqrt(jnp.mean(x * x, axis=-1, keepdims=True) + EPS) * g


def _lam_init(layer_idx):
    return 0.8 - 0.6 * math.exp(-0.3 * layer_idx)


def _lam(lq1, lk1, lq2, lk2, lam_init):
    a = jnp.sum(lq1[...] * lk1[...], keepdims=True)
    b = jnp.sum(lq2[...] * lk2[...], keepdims=True)
    return jnp.exp(a) - jnp.exp(b) + lam_init


def _rope_tables(pos):
    inv = ROPE_THETA ** (-jnp.arange(0, ROT_DIM, 2, dtype=F32) / ROT_DIM)
    ang = pos.astype(F32)[:, None] * inv[None, :]
    d = jnp.arange(2 * HEAD_DIM) % HEAD_DIM
    f = d % HALF_ROT
    cos_f = jnp.cos(ang)[:, f]
    sin_f = jnp.sin(ang)[:, f]
    lo = (d < HALF_ROT)[None, :]
    hi = ((d >= HALF_ROT) & (d < ROT_DIM))[None, :]
    cos_t = jnp.where(lo | hi, cos_f, 1.0)
    s_next = jnp.where(lo, -sin_f, 0.0)
    s_prev = jnp.where(hi, sin_f, 0.0)
    return cos_t, s_next, s_prev


def _inproj_kernel(x_ref, gn_ref, w_ref, cos_ref, snext_ref, sprev_ref,
                   q_ref, k_ref, v_ref, sa_ref, g_ref, sc_ref, u_scr, ca_scr):
    n = pl.program_id(1)
    seg = w_ref.shape[1]

    def segment(epilogue):
        u = u_scr[...]
        for c0 in range(0, seg, INPROJ_CHUNK):
            cols = slice(c0, c0 + INPROJ_CHUNK)
            epilogue(jnp.dot(u, w_ref[:, cols], preferred_element_type=F32), cols)

    def rope_to(out_ref, scale):
        def epilogue(z, cols):
            cos_t, s_next, s_prev = cos_ref[...], snext_ref[...], sprev_ref[...]
            for h in range(z.shape[1] // LANES):
                zh = z[:, h * LANES:(h + 1) * LANES]
                nxt = pltpu.roll(zh, LANES - HALF_ROT, 1)
                prv = pltpu.roll(zh, HALF_ROT, 1)
                r = zh * cos_t + nxt * s_next + prv * s_prev
                if scale != 1.0:
                    r = r * scale
                out_ref[:, cols.start + h * LANES:cols.start + (h + 1) * LANES] = r.astype(out_ref.dtype)
        return epilogue

    def store_to(out_ref, fn):
        def epilogue(z, cols):
            out_ref[:, cols] = fn(z, cols).astype(out_ref.dtype)
        return epilogue

    @pl.when(n == 0)
    def _q():
        u_scr[...] = _rms(x_ref[...], gn_ref[...]).astype(BF16)
        segment(rope_to(q_ref, Q_SCALE))

    @pl.when(n == 1)
    def _k():
        segment(rope_to(k_ref, 1.0))

    @pl.when(n == 2)
    def _v():
        segment(store_to(v_ref, lambda z, cols: z))

    @pl.when(n == 3)
    def _gate_a():
        segment(store_to(sa_ref, lambda z, cols: _silu(z)))

    @pl.when(n == 4)
    def _glu_a():
        segment(store_to(ca_scr, lambda z, cols: z))

    @pl.when(n == 5)
    def _glu_b():
        segment(store_to(g_ref, lambda z, cols: ca_scr[:, cols] * _sigmoid(z)))

    @pl.when(n == 6)
    def _gate_c():
        segment(store_to(sc_ref, lambda z, cols: _silu(z)))


def _inproj(x2d, g_norm, w_in_bf, tables, tm, seg):
    m, d = x2d.shape
    n_seg = w_in_bf.shape[1] // seg
    assert n_seg == 7 and m % tm == 0
    cos_t, s_next, s_prev = tables
    t_blocks = cos_t.shape[0] // tm
    row = lambda i, n: (i, 0)
    tab = lambda i, n: (i % t_blocks, 0)
    out_sds = lambda dt: jax.ShapeDtypeStruct((m, seg), dt)
    return pl.pallas_call(
        _inproj_kernel,
        grid=(m // tm, n_seg),
        in_specs=[
            pl.BlockSpec((tm, d), row),
            pl.BlockSpec((1, d), lambda i, n: (0, 0)),
            pl.BlockSpec((d, seg), lambda i, n: (0, n)),
            pl.BlockSpec((tm, LANES), tab),
            pl.BlockSpec((tm, LANES), tab),
            pl.BlockSpec((tm, LANES), tab),
        ],
        out_specs=[pl.BlockSpec((tm, seg), row)] * 6,
        out_shape=[out_sds(BF16), out_sds(F32), out_sds(F32), out_sds(BF16), out_sds(F32), out_sds(BF16)],
        scratch_shapes=[pltpu.VMEM((tm, d), BF16), pltpu.VMEM((tm, seg), F32)],
        compiler_params=pltpu.CompilerParams(
            dimension_semantics=("parallel", "arbitrary"), vmem_limit_bytes=VMEM_LIMIT),
        name="inproj",
    )(x2d, g_norm.reshape(1, d), w_in_bf, cos_t, s_next, s_prev)


class _DecodeHost:
    def __init__(self, pt_ref, lam, gs, qd_ref, knd_ref, vnd_ref, sad_ref, ck_hbm, cv_hbm, od_ref,
                 kbuf, vbuf, sems, m_scr, l_scr, acca_scr, accb_scr,
                 *, lam_init, layer, tok0, step, n_steps):
        self.pt_ref, self.lam, self.gs, self.lam_init, self.layer = pt_ref, lam, gs, lam_init, layer
        self.knd_ref, self.vnd_ref, self.sad_ref, self.od_ref = knd_ref, vnd_ref, sad_ref, od_ref
        self.ck_hbm, self.cv_hbm, self.kbuf, self.vbuf, self.sems = ck_hbm, cv_hbm, kbuf, vbuf, sems
        self.state = (m_scr, l_scr, acca_scr, accb_scr)
        self.tok0, self.step, self.n_steps = tok0, step, n_steps
        self.n_tok, self.nh, self.hw = qd_ref.shape
        self.n_parts, self.part = kbuf.shape[0], kbuf.shape[1]
        self.per_tok = self.n_parts * self.part // self.n_tok
        assert self.per_tok * self.n_tok == self.n_parts * self.part
        assert self.per_tok * n_steps == pt_ref.shape[1]
        hw = self.hw
        d_map = lax.broadcasted_iota(jnp.int32, (hw, hw), 0) // HEAD_DIM
        n_map = lax.broadcasted_iota(jnp.int32, (hw, hw), 1) // HEAD_DIM
        self.sel = (d_map == n_map).astype(BF16)
        self.qd = qd_ref[...].astype(F32)

    def _page_copies(self, step_idx, hb, i):
        j = hb * self.part + i
        pid = self.pt_ref[self.tok0 + j // self.per_tok, step_idx * self.per_tok + j % self.per_tok]
        return (pltpu.make_async_copy(self.ck_hbm.at[self.layer, pid], self.kbuf.at[hb, i], self.sems.at[0, hb, i]),
                pltpu.make_async_copy(self.cv_hbm.at[self.layer, pid], self.vbuf.at[hb, i], self.sems.at[1, hb, i]))

    def _start_part(self, step_idx, hb):
        for i in range(self.part):
            for copy in self._page_copies(step_idx, hb, i):
                copy.start()

    def wait_part(self, hb):
        for i in range(self.part):
            for copy in self._page_copies(self.step, hb, i):
                copy.wait()

    def refill_part(self, hb):
        @pl.when(self.step + 1 < self.n_steps)
        def _():
            self._start_part(self.step + 1, hb)

    def begin(self):
        @pl.when(self.step == 0)
        def _():
            for hb in range(self.n_parts):
                self._start_part(0, hb)
            m_scr, l_scr, acca_scr, accb_scr = self.state
            m_scr[...] = jnp.full(m_scr.shape, NEG_INF, F32)
            l_scr[...] = jnp.zeros(l_scr.shape, F32)
            acca_scr[...] = jnp.zeros(acca_scr.shape, F32)
            accb_scr[...] = jnp.zeros(accb_scr.shape, F32)

    @staticmethod
    def _swap_maps(x):
        return pltpu.roll(x, HEAD_DIM, x.ndim - 1)

    @staticmethod
    def _reduce(fn, x):
        ways = DECODE_REDUCE_WAYS if x.shape[0] % DECODE_REDUCE_WAYS == 0 else 1
        return fn(fn(x.reshape(ways, x.shape[0] // ways, *x.shape[1:]), axis=1), axis=0)

    def _scores(self, tok, kp):
        rows = kp.shape[0]
        prod = (kp * self.qd[tok][None]).reshape(rows * self.nh, self.hw)
        return jnp.dot(prod.astype(BF16), self.sel, preferred_element_type=F32)

    def _update(self, t, s, vp):
        m_scr, l_scr, acca_scr, accb_scr = self.state
        nh, hw = self.nh, self.hw
        rows = vp.shape[0]
        m_old = m_scr[t]
        m_new = jnp.maximum(m_old, self._reduce(jnp.max, s.reshape(rows, nh, hw)))
        alpha = jnp.exp2(m_old - m_new)
        p2 = jnp.exp2(s.reshape(rows, nh, hw) - m_new[None]).reshape(rows * nh, hw)
        p = p2.reshape(rows, nh, hw)
        p_sw = self._swap_maps(p2).reshape(rows, nh, hw)
        l_scr[t] = alpha * l_scr[t] + self._reduce(jnp.sum, p)
        acca_scr[t] = alpha * acca_scr[t] + self._reduce(jnp.sum, p * vp)
        accb_scr[t] = self._swap_maps(alpha) * accb_scr[t] + self._reduce(jnp.sum, p_sw * vp)
        m_scr[t] = m_new

    def page_scores(self, hb, i):
        return self._scores((hb * self.part + i) // self.per_tok, self.kbuf[hb, i])

    def page_update(self, hb, i, s):
        self._update(hb * self.part + i, s, self.vbuf[hb, i])

    def finish(self):
        @pl.when(self.step == self.n_steps - 1)
        def _():
            m_scr, l_scr, acca_scr, accb_scr = self.state
            nh, hw, per_tok = self.nh, self.hw, self.per_tok
            lo = lax.broadcasted_iota(jnp.int32, (nh, hw), 1) < HEAD_DIM
            for tok in range(self.n_tok):
                t0 = tok * per_tok
                self._update(t0, self._scores(tok, self.knd_ref[tok][None]), self.vnd_ref[tok][None])
                m_all = m_scr[t0:t0 + per_tok]
                m = jnp.max(m_all, axis=0)
                w = jnp.exp2(m_all - m[None])
                w_sw = self._swap_maps(w.reshape(per_tok * nh, hw)).reshape(per_tok, nh, hw)
                l = jnp.sum(w * l_scr[t0:t0 + per_tok], axis=0)
                acc_a = jnp.sum(w * acca_scr[t0:t0 + per_tok], axis=0)
                acc_b = jnp.sum(w_sw * accb_scr[t0:t0 + per_tok], axis=0)
                na = acc_a / l
                nb = acc_b / self._swap_maps(l)
                o = jnp.where(lo, na, nb) - self.lam * jnp.where(lo, nb, na)
                o = _rms(o, self.gs) * (1.0 - self.lam_init)
                self.od_ref[tok] = o * self.sad_ref[tok].astype(F32)


def _decode_host_specs(n_tok, tok0, nh, hw, page, pages_per_step):
    assert tok0 % n_tok == 0 and pages_per_step % N_HALVES == 0
    tok_spec = pl.BlockSpec((n_tok, nh, hw), lambda *_: (tok0 // n_tok, 0, 0))
    part = pages_per_step // N_HALVES
    scratch = [
        pltpu.VMEM((N_HALVES, part, page, nh, hw), F32), pltpu.VMEM((N_HALVES, part, page, nh, hw), F32),
        pltpu.SemaphoreType.DMA((2, N_HALVES, part)),
    ] + [pltpu.VMEM((pages_per_step, nh, hw), F32)] * 4
    return tok_spec, pl.BlockSpec(memory_space=pl.ANY), scratch


def _attn_kernel(pt_ref, lq1, lk1, lq2, lk2, gs_ref, q_ref, k_ref, v_ref, sa_ref,
                 qd_ref, knd_ref, vnd_ref, sad_ref, ck_hbm, cv_hbm, o_ref, od_ref,
                 kt_scr, vb_scr, kbuf, vbuf, sems, m_scr, l_scr, acca_scr, accb_scr,
                 *, tq, lam_init, layer, tok0, n_steps):
    s_len, hw = k_ref.shape
    step = pl.program_id(0) * pl.num_programs(1) + pl.program_id(1)
    lam = _lam(lq1, lk1, lq2, lk2, lam_init)
    gs = gs_ref[...]
    dec = _DecodeHost(pt_ref, lam, gs, qd_ref, knd_ref, vnd_ref, sad_ref, ck_hbm, cv_hbm, od_ref,
                      kbuf, vbuf, sems, m_scr, l_scr, acca_scr, accb_scr,
                      lam_init=lam_init, layer=layer, tok0=tok0, step=step, n_steps=n_steps)
    dec.begin()

    kt_scr[...] = k_ref[...].T.astype(BF16)
    vb_scr[:, 0:hw] = v_ref[...].astype(BF16)
    vb_scr[:, hw:2 * hw] = jnp.ones((s_len, hw), BF16)
    lane = lax.broadcasted_iota(jnp.int32, (tq, hw), 1)
    q_row = lax.broadcasted_iota(jnp.int32, (2 * tq, tq), 0) % tq
    k_col = lax.broadcasted_iota(jnp.int32, (2 * tq, tq), 1)
    causal = k_col <= q_row

    def attention_tile(i):
        r0, c_diag = i * tq, i * tq
        q = q_ref[r0:r0 + tq, :]
        zero = jnp.zeros_like(q)
        q2 = jnp.concatenate(
            [jnp.where(lane < HEAD_DIM, q, zero), jnp.where(lane >= HEAD_DIM, q, zero)], axis=0)
        s_diag = jnp.dot(q2, kt_scr[:, c_diag:c_diag + tq], preferred_element_type=F32)
        s_diag = jnp.where(causal, s_diag, NEG_INF)
        m = jnp.max(s_diag, axis=-1, keepdims=True)
        if i > 0:
            s_past = jnp.dot(q2, kt_scr[:, 0:c_diag], preferred_element_type=F32)
            m = jnp.maximum(m, jnp.max(s_past, axis=-1, keepdims=True))
        acc = jnp.dot(jnp.exp2(s_diag - m).astype(BF16), vb_scr[c_diag:c_diag + tq, :],
                      preferred_element_type=F32)
        if i > 0:
            acc = acc + jnp.dot(jnp.exp2(s_past - m).astype(BF16), vb_scr[0:c_diag, :],
                                preferred_element_type=F32)
        o = acc[0:tq, 0:hw] / acc[0:tq, hw:] - lam * (acc[tq:, 0:hw] / acc[tq:, hw:])
        o = _rms(o, gs) * (1.0 - lam_init)
        o_ref[r0:r0 + tq, :] = (o * sa_ref[r0:r0 + tq, :].astype(F32)).astype(o_ref.dtype)

    def part_of_step(q_tiles, hb):
        work = sum(i + 1 for i in q_tiles)
        work_done = pages_done = 0
        for i in q_tiles:
            attention_tile(i)
            work_done += i + 1
            while pages_done < work_done * dec.part // work:
                dec.page_update(hb, pages_done, dec.page_scores(hb, pages_done))
                pages_done += 1

    n_q = s_len // tq
    tiles_upto = lambda n: n * (n + 1) // 2
    split = min(range(n_q + 1), key=lambda n: abs(2 * tiles_upto(n) - tiles_upto(n_q)))
    for hb, q_tiles in enumerate((range(0, split), range(split, n_q))):
        dec.wait_part(hb)
        part_of_step(q_tiles, hb)
        dec.refill_part(hb)
    dec.finish()


def _attention(q, k, v, sa, dec_ops, caches, page_table, layer, tok0, n_tok, lams, g_subln, tq, lam_init):
    b, s, aw = q.shape
    _, nh, hw = dec_ops[0].shape
    n_pages, page = page_table.shape[1], caches[0].shape[2]
    n_steps = b * nh
    assert aw == nh * hw and (n_tok * n_pages) % n_steps == 0
    pages_per_step = n_tok * n_pages // n_steps
    head = lambda bi, h, pt: (bi, 0, h)
    small = lambda bi, h, pt: (0, 0)
    head_spec = pl.BlockSpec((None, s, hw), head)
    tok_spec, hbm_spec, dec_scratch = _decode_host_specs(n_tok, tok0, nh, hw, page, pages_per_step)
    grid_spec = pltpu.PrefetchScalarGridSpec(
        num_scalar_prefetch=1,
        grid=(b, nh),
        in_specs=[pl.BlockSpec((1, HEAD_DIM), small)] * 4 + [pl.BlockSpec((1, hw), small)]
        + [head_spec] * 4 + [tok_spec] * 4 + [hbm_spec] * 2,
        out_specs=[head_spec, pl.BlockSpec((n_tok, nh, hw), lambda bi, h, pt: (0, 0, 0))],
        scratch_shapes=[pltpu.VMEM((hw, s), BF16), pltpu.VMEM((s, 2 * hw), BF16)] + dec_scratch,
    )
    return pl.pallas_call(
        functools.partial(_attn_kernel, tq=tq, lam_init=lam_init, layer=layer, tok0=tok0, n_steps=n_steps),
        grid_spec=grid_spec,
        out_shape=[jax.ShapeDtypeStruct((b, s, aw), BF16), jax.ShapeDtypeStruct((n_tok, nh, hw), F32)],
        compiler_params=pltpu.CompilerParams(
            dimension_semantics=("arbitrary", "arbitrary"), vmem_limit_bytes=VMEM_LIMIT),
        name="attention",
    )(page_table, *lams, g_subln.reshape(1, hw), q, k, v, sa, *dec_ops, *caches)


def _conv_kernel(gc_ref, gh_ref, wdw_ref, bdw_ref, gcl_ref, bcl_ref, wpw_ref, sc_ref, o_ref,
                 sh_scr, c_scr, *, ts, tc, halo, kw, rc):
    i = pl.program_id(1)
    c = pl.program_id(2)
    nc = c_scr.shape[0]
    off = halo - (kw - 1)
    n_copy = halo + ts - SUBLANES
    sh_scr[0, 0:halo, :] = jnp.where(i == 0, 0.0, gh_ref[...])
    sh_scr[0, halo:halo + ts, :] = gc_ref[...]
    for ph in range(1, SUBLANES):
        sh_scr[ph, 0:n_copy, :] = sh_scr[0, ph:ph + n_copy, :]
    for r in range(ts // rc):
        acc = jnp.broadcast_to(bdw_ref[...], (rc, tc))
        for j in range(kw):
            ph, base = (off + j) % SUBLANES, (off + j) // SUBLANES * SUBLANES
            acc = acc + wdw_ref[j:j + 1, :] * sh_scr[ph, r * rc + base:r * rc + base + rc, :]
        c_scr[c, r * rc:(r + 1) * rc, :] = acc

    @pl.when(c == nc - 1)
    def _norm_pw():
        width = nc * tc
        cs = [c_scr[k] for k in range(nc)]
        mu = sum(jnp.sum(x, axis=-1, keepdims=True) for x in cs) / width
        var = sum(jnp.sum(jnp.square(x - mu), axis=-1, keepdims=True) for x in cs) / width
        inv = lax.rsqrt(var + EPS)
        acc = jnp.zeros(o_ref.shape, F32)
        for k in range(nc):
            y = (cs[k] - mu) * inv * gcl_ref[:, k * tc:(k + 1) * tc] + bcl_ref[:, k * tc:(k + 1) * tc]
            acc = acc + jnp.dot(_silu(y).astype(BF16), wpw_ref[k * tc:(k + 1) * tc, :],
                                preferred_element_type=F32)
        o_ref[...] = (acc * sc_ref[...].astype(F32)).astype(o_ref.dtype)


def _prompt_conv(g, sc, w_dw, b_dw, g_cln, b_cln, w_pw_bf, ts, tc):
    b, s, cw = g.shape
    kw = w_dw.shape[0]
    halo = -(-(kw - 1) // SUBLANES) * SUBLANES
    assert ts % halo == 0 and s % ts == 0 and cw % tc == 0
    hb = ts // halo
    vec = lambda a: a.reshape(1, cw)
    full = lambda bi, i, c: (0, 0)
    return pl.pallas_call(
        functools.partial(_conv_kernel, ts=ts, tc=tc, halo=halo, kw=kw, rc=64),
        grid=(b, s // ts, cw // tc),
        in_specs=[
            pl.BlockSpec((None, ts, tc), lambda bi, i, c: (bi, i, c)),
            pl.BlockSpec((None, halo, tc), lambda bi, i, c: (bi, jnp.maximum(i * hb - 1, 0), c)),
            pl.BlockSpec((kw, tc), lambda bi, i, c: (0, c)),
            pl.BlockSpec((1, tc), lambda bi, i, c: (0, c)),
            pl.BlockSpec((1, cw), full),
            pl.BlockSpec((1, cw), full),
            pl.BlockSpec((cw, cw), full),
            pl.BlockSpec((None, ts, cw), lambda bi, i, c: (bi, i, 0)),
        ],
        out_specs=pl.BlockSpec((None, ts, cw), lambda bi, i, c: (bi, i, 0)),
        out_shape=jax.ShapeDtypeStruct((b, s, cw), BF16),
        scratch_shapes=[pltpu.VMEM((SUBLANES, halo + ts, tc), F32),
                        pltpu.VMEM((cw // tc, ts, tc), F32)],
        compiler_params=pltpu.CompilerParams(
            dimension_semantics=("parallel", "parallel", "arbitrary"), vmem_limit_bytes=VMEM_LIMIT),
        name="prompt_conv",
    )(g, g, w_dw, vec(b_dw), vec(g_cln), vec(b_cln), w_pw_bf, sc)


def _decode_conv_kernel(hist_ref, wdw_ref, bdw_ref, gcl_ref, bcl_ref, wpw_ref, sc_ref, o_ref):
    kw = hist_ref.shape[0]
    acc = jnp.broadcast_to(bdw_ref[...], o_ref.shape)
    for j in range(kw):
        acc = acc + wdw_ref[j:j + 1, :] * hist_ref[j]
    mu = jnp.mean(acc, axis=-1, keepdims=True)
    var = jnp.mean(jnp.square(acc - mu), axis=-1, keepdims=True)
    y = (acc - mu) * lax.rsqrt(var + EPS) * gcl_ref[...] + bcl_ref[...]
    c = jnp.dot(_silu(y).astype(BF16), wpw_ref[...], preferred_element_type=F32)
    o_ref[...] = (c * sc_ref[...].astype(F32)).astype(o_ref.dtype)


def _decode_conv(hist_t, sc, w_dw, b_dw, g_cln, b_cln, w_pw_bf):
    kw, db, cw = hist_t.shape
    vec = lambda a: a.reshape(1, cw)
    return pl.pallas_call(
        _decode_conv_kernel,
        out_shape=jax.ShapeDtypeStruct((db, cw), BF16),
        compiler_params=pltpu.CompilerParams(vmem_limit_bytes=VMEM_LIMIT),
        name="decode_conv",
    )(hist_t, w_dw, vec(b_dw), vec(g_cln), vec(b_cln), w_pw_bf, sc)


def _out_rows(rows, x_ref, oa_ref, oc_ref, wout_ref):
    aw = oa_ref.shape[1]
    h = x_ref[rows, :]
    h = h + jnp.dot(oa_ref[rows, :], wout_ref[0:aw, :], preferred_element_type=F32)
    return h + jnp.dot(oc_ref[rows, :], wout_ref[aw:, :], preferred_element_type=F32)


def _ple_rows(rows, h, p_ref, gple_ref, wpg_ref, wple_ref, gfin_ref, y_ref, final_norm):
    r = _rms(h, gple_ref[...]).astype(BF16)
    gate = _sigmoid(jnp.dot(r, wpg_ref[...], preferred_element_type=F32))
    e = jnp.dot(p_ref[rows, :].astype(BF16), wple_ref[...], preferred_element_type=F32)
    h = h + gate * e
    y_ref[rows, :] = _rms(h, gfin_ref[...]) if final_norm else h


def _out_kernel(x_ref, oa_ref, oc_ref, p_ref, wout_ref, gple_ref, wpg_ref, wple_ref, gfin_ref, y_ref,
                *, final_norm):
    tm = oa_ref.shape[0]
    sub = min(tm, OUT_SUBTILE)
    for r0 in range(0, tm, sub):
        rows = slice(r0, r0 + sub)
        h = _out_rows(rows, x_ref, oa_ref, oc_ref, wout_ref)
        _ple_rows(rows, h, p_ref, gple_ref, wpg_ref, wple_ref, gfin_ref, y_ref, final_norm)


def _out_decode_kernel(pt_ref, lq1, lk1, lq2, lk2, gs_ref,
                       x_ref, oa_ref, oc_ref, p_ref, wout_ref, gple_ref, wpg_ref, wple_ref, gfin_ref,
                       qd_ref, knd_ref, vnd_ref, sad_ref, ck_hbm, cv_hbm, y_ref, od_ref,
                       kbuf, vbuf, sems, m_scr, l_scr, acca_scr, accb_scr,
                       *, final_norm, lam_init, layer, tok0, n_steps):
    rows = slice(0, oa_ref.shape[0])
    dec = _DecodeHost(pt_ref, _lam(lq1, lk1, lq2, lk2, lam_init), gs_ref[...],
                      qd_ref, knd_ref, vnd_ref, sad_ref, ck_hbm, cv_hbm, od_ref,
                      kbuf, vbuf, sems, m_scr, l_scr, acca_scr, accb_scr,
                      lam_init=lam_init, layer=layer, tok0=tok0, step=pl.program_id(0), n_steps=n_steps)
    dec.begin()
    dec.wait_part(0)
    scores = [dec.page_scores(0, i) for i in range(dec.part)]
    h = _out_rows(rows, x_ref, oa_ref, oc_ref, wout_ref)
    for i in range(dec.part):
        dec.page_update(0, i, scores[i])
    dec.refill_part(0)
    dec.wait_part(1)
    scores = [dec.page_scores(1, i) for i in range(dec.part)]
    _ple_rows(rows, h, p_ref, gple_ref, wpg_ref, wple_ref, gfin_ref, y_ref, final_norm)
    for i in range(dec.part):
        dec.page_update(1, i, scores[i])
    dec.refill_part(1)
    dec.finish()


def _outproj(x2d, oa, oc, p2d, w_out_bf, g_ple, w_pg_bf, w_ple_bf, g_final, tm, final_norm, decode=None):
    m, d = x2d.shape
    aw, cw, pd = oa.shape[1], oc.shape[1], p2d.shape[1]
    row = lambda i, *_: (i, 0)
    full = lambda i, *_: (0, 0)
    once = pl.Buffered(1)
    in_specs = [
        pl.BlockSpec((tm, d), row),
        pl.BlockSpec((tm, aw), row),
        pl.BlockSpec((tm, cw), row),
        pl.BlockSpec((tm, pd), row),
        pl.BlockSpec((aw + cw, d), full, pipeline_mode=once),
        pl.BlockSpec((1, d), full),
        pl.BlockSpec((d, d), full, pipeline_mode=once),
        pl.BlockSpec((pd, d), full, pipeline_mode=once),
        pl.BlockSpec((1, d), full),
    ]
    operands = (x2d, oa, oc, p2d, w_out_bf, g_ple.reshape(1, d), w_pg_bf, w_ple_bf, g_final.reshape(1, d))
    y_spec, y_shape = pl.BlockSpec((tm, d), row), jax.ShapeDtypeStruct((m, d), F32)
    if decode is None:
        return pl.pallas_call(
            functools.partial(_out_kernel, final_norm=final_norm),
            grid=(m // tm,), in_specs=in_specs, out_specs=y_spec, out_shape=y_shape,
            compiler_params=pltpu.CompilerParams(
                dimension_semantics=("parallel",), vmem_limit_bytes=VMEM_LIMIT),
            name="outproj",
        )(*operands)
    dec_ops, caches, page_table, layer, tok0, n_tok, lams, g_subln, lam_init = decode
    _, nh, hw = dec_ops[0].shape
    n_pages, page = page_table.shape[1], caches[0].shape[2]
    n_steps = m // tm
    assert (n_tok * n_pages) % n_steps == 0
    tok_spec, hbm_spec, dec_scratch = _decode_host_specs(n_tok, tok0, nh, hw, page, n_tok * n_pages // n_steps)
    grid_spec = pltpu.PrefetchScalarGridSpec(
        num_scalar_prefetch=1,
        grid=(n_steps,),
        in_specs=[pl.BlockSpec((1, HEAD_DIM), full)] * 4 + [pl.BlockSpec((1, hw), full)] + in_specs
        + [tok_spec] * 4 + [hbm_spec] * 2,
        out_specs=[y_spec, pl.BlockSpec((n_tok, nh, hw), lambda i, pt: (0, 0, 0))],
        scratch_shapes=dec_scratch,
    )
    return pl.pallas_call(
        functools.partial(_out_decode_kernel, final_norm=final_norm, lam_init=lam_init, layer=layer,
                          tok0=tok0, n_steps=n_steps),
        grid_spec=grid_spec,
        out_shape=[y_shape, jax.ShapeDtypeStruct((n_tok, nh, hw), F32)],
        compiler_params=pltpu.CompilerParams(
            dimension_semantics=("arbitrary",), vmem_limit_bytes=VMEM_LIMIT),
        name="outproj_decode",
    )(page_table, *lams, g_subln.reshape(1, hw), *operands, *dec_ops, *caches)


def kernel(x_prompt, x_sample, p_prompt, p_sample, cache_k, cache_v, state_conv, page_table, w_in, g_norm, lam_q1, lam_k1, lam_q2, lam_k2, g_subln, w_dw, b_dw, g_cln, b_cln, w_pw, w_out, g_ple, w_pg, w_ple, g_final):
    b, s, d = x_prompt.shape
    db, ds, _ = x_sample.shape
    depth = w_in.shape[0]
    nh, hw = cache_k.shape[3], cache_k.shape[4]
    aw = nh * hw
    cw = w_dw.shape[2]
    kw = w_dw.shape[1]
    past = page_table.shape[1] * cache_k.shape[2]
    assert ds == 1 and aw == cw and hw == 2 * HEAD_DIM

    tab_p = _rope_tables(jnp.arange(s, dtype=jnp.int32))
    tab_s = _rope_tables(jnp.full((db * ds,), past, dtype=jnp.int32))

    h_p = x_prompt.reshape(b * s, d)
    h_s = x_sample.reshape(db * ds, d)
    outs = {k: [] for k in ("kp", "vp", "cp", "ks", "vs", "cs")}
    for i in range(depth):
        lam_init = _lam_init(i)
        last = i == depth - 1
        lams = tuple(a[i].reshape(1, HEAD_DIM) for a in (lam_q1, lam_k1, lam_q2, lam_k2))
        w_in_bf, w_pw_bf = w_in[i].astype(BF16), w_pw[i].astype(BF16)
        w_out_bf, w_pg_bf, w_ple_bf = w_out[i].astype(BF16), w_pg[i].astype(BF16), w_ple[i].astype(BF16)

        qd, kd, vd, sad, gd, scd = _inproj(h_s, g_norm[i], w_in_bf, tab_s, tm=db * ds, seg=aw)
        q, k, v, sa, g, sc = _inproj(h_p, g_norm[i], w_in_bf, tab_p, tm=512, seg=aw)
        as3 = lambda a: a.reshape(b, s, -1)
        dec_ops = tuple(a.reshape(db, nh, hw) for a in (qd, kd, vd, sad))
        caches = (cache_k, cache_v)
        n_att = db // 2

        oa, oad_a = _attention(as3(q), as3(k), as3(v), as3(sa), dec_ops, caches, page_table, i, 0, n_att,
                               lams, g_subln[i], tq=256, lam_init=lam_init)
        oc = _prompt_conv(as3(g), as3(sc), w_dw[i], b_dw[i], g_cln[i], b_cln[i], w_pw_bf, ts=256, tc=256)
        h_p, oad_o = _outproj(h_p, oa.reshape(b * s, aw), oc.reshape(b * s, cw), p_prompt[i].reshape(b * s, -1),
                              w_out_bf, g_ple[i], w_pg_bf, w_ple_bf, g_final, tm=256, final_norm=last,
                              decode=(dec_ops, caches, page_table, i, n_att, db - n_att, lams, g_subln[i], lam_init))
        oad = jnp.concatenate([oad_a, oad_o], axis=0)
        outs["kp"].append(k.reshape(b, s, nh, hw))
        outs["vp"].append(v.reshape(b, s, nh, hw))
        outs["cp"].append(as3(g)[:, s - (kw - 1):])

        hist = jnp.concatenate([state_conv[i], gd.reshape(db, ds, cw)], axis=1)
        ocd = _decode_conv(hist.transpose(1, 0, 2), scd, w_dw[i], b_dw[i], g_cln[i], b_cln[i], w_pw_bf)
        h_s = _outproj(h_s, oad.reshape(db, aw).astype(BF16), ocd, p_sample[i].reshape(db * ds, -1),
                       w_out_bf, g_ple[i], w_pg_bf, w_ple_bf, g_final, tm=db * ds, final_norm=last)
        outs["ks"].append(kd.reshape(db, ds, nh, hw))
        outs["vs"].append(vd.reshape(db, ds, nh, hw))
        outs["cs"].append(hist[:, 1:])

    st = lambda key: jnp.stack(outs[key])
    return (h_p.reshape(b, s, d), h_s.reshape(db, ds, d),
            st("kp"), st("vp"), st("cp"), st("ks"), st("vs"), st("cs"))
```

```python
import functools
import math

import jax
import jax.numpy as jnp
from jax import lax
from jax.experimental import pallas as pl
from jax.experimental.pallas import tpu as pltpu

F32 = jnp.float32
BF16 = jnp.bfloat16

HEAD_DIM = 64
ROT_DIM = HEAD_DIM // 4
HALF_ROT = ROT_DIM // 2
ROPE_THETA = 500000.0
EPS = 1e-6
NEG_INF = -1e30
Q_SCALE = HEAD_DIM ** -0.5 * math.log2(math.e)

LANES = 128
SUBLANES = 8
VMEM_LIMIT = 56 * 1024 * 1024
MXU_DIM = 256
INPROJ_CHUNK = MXU_DIM
OUT_SUBTILE = 256
N_HALVES = 2
DECODE_REDUCE_WAYS = 4


def _sigmoid(x):
    return 1.0 / (1.0 + jnp.exp(-x))


def _silu(x):
    return x * _sigmoid(x)


def _rms(x, g):
    return x * lax.rsqrt(jnp.mean(x * x, axis=-1, keepdims=True) + EPS) * g


def _lam_init(layer_idx):
    return 0.8 - 0.6 * math.exp(-0.3 * layer_idx)


def _lam(lq1, lk1, lq2, lk2, lam_init):
    a = jnp.sum(lq1[...] * lk1[...], keepdims=True)
    b = jnp.sum(lq2[...] * lk2[...], keepdims=True)
    return jnp.exp(a) - jnp.exp(b) + lam_init


def _rope_tables(pos):
    inv = ROPE_THETA ** (-jnp.arange(0, ROT_DIM, 2, dtype=F32) / ROT_DIM)
    ang = pos.astype(F32)[:, None] * inv[None, :]
    d = jnp.arange(2 * HEAD_DIM) % HEAD_DIM
    f = d % HALF_ROT
    cos_f = jnp.cos(ang)[:, f]
    sin_f = jnp.sin(ang)[:, f]
    lo = (d < HALF_ROT)[None, :]
    hi = ((d >= HALF_ROT) & (d < ROT_DIM))[None, :]
    cos_t = jnp.where(lo | hi, cos_f, 1.0)
    s_next = jnp.where(lo, -sin_f, 0.0)
    s_prev = jnp.where(hi, sin_f, 0.0)
    return cos_t, s_next, s_prev


SEGMENTS = ("q", "k", "v", "gate_a", "glu_a", "glu_b", "gate_c")
SEGMENT_DTYPE = {"q": BF16, "k": F32, "v": F32, "gate_a": BF16, "glu_b": F32, "gate_c": BF16}


def _inproj_kernel(*refs, kinds):
    (x_ref, xd_ref, gn_ref, w_ref, cos_ref, snext_ref, sprev_ref, cosd_ref, snextd_ref, sprevd_ref), refs = refs[:10], refs[10:]
    stored = [k for k in kinds if k in SEGMENT_DTYPE]
    out_p = dict(zip(stored, refs[:len(stored)]))
    out_d = dict(zip(stored, refs[len(stored):2 * len(stored)]))
    u_scr, *glu_scr = refs[2 * len(stored):]
    n = pl.program_id(1)
    tm, seg = x_ref.shape[0], w_ref.shape[1]

    def segment(epilogue):
        u = u_scr[...]
        for c0 in range(0, seg, INPROJ_CHUNK):
            cols = slice(c0, c0 + INPROJ_CHUNK)
            z = jnp.dot(u, w_ref[:, cols], preferred_element_type=F32)
            epilogue(z, cols)

    def rope(z, scale):
        tables = [jnp.concatenate([t[...], td[...]], axis=0)
                  for t, td in ((cos_ref, cosd_ref), (snext_ref, snextd_ref), (sprev_ref, sprevd_ref))]
        cos_t, s_next, s_prev = tables
        heads = []
        for h in range(z.shape[1] // LANES):
            zh = z[:, h * LANES:(h + 1) * LANES]
            nxt = pltpu.roll(zh, LANES - HALF_ROT, 1)
            prv = pltpu.roll(zh, HALF_ROT, 1)
            heads.append((zh * cos_t + nxt * s_next + prv * s_prev) * scale)
        return jnp.concatenate(heads, axis=1)

    def store(kind, fn):
        def epilogue(z, cols):
            r = fn(z, cols)
            out_p[kind][:, cols] = r[0:tm].astype(out_p[kind].dtype)
            out_d[kind][:, cols] = r[tm:].astype(out_d[kind].dtype)
        return epilogue

    epilogues = {
        "q": store("q", lambda z, cols: rope(z, Q_SCALE)),
        "k": store("k", lambda z, cols: rope(z, 1.0)),
        "v": store("v", lambda z, cols: z),
        "gate_a": store("gate_a", lambda z, cols: _silu(z)),
        "gate_c": store("gate_c", lambda z, cols: _silu(z)),
        "glu_b": store("glu_b", lambda z, cols: glu_scr[0][:, cols] * _sigmoid(z)),
    }

    def keep_glu_a(z, cols):
        glu_scr[0][:, cols] = z

    for idx, kind in enumerate(kinds):
        @pl.when(n == idx)
        def _(idx=idx, kind=kind):
            if idx == 0:
                u_scr[0:tm, :] = _rms(x_ref[...], gn_ref[...]).astype(BF16)
                u_scr[tm:, :] = _rms(xd_ref[...], gn_ref[...]).astype(BF16)
            segment(keep_glu_a if kind == "glu_a" else epilogues[kind])


def _inproj(x2d, xd2d, g_norm, w_in_bf, tab_p, tab_d, tm, seg, kinds):
    m, d = x2d.shape
    md = xd2d.shape[0]
    seg0 = SEGMENTS.index(kinds[0])
    assert SEGMENTS[seg0:seg0 + len(kinds)] == tuple(kinds) and m % tm == 0 and w_in_bf.shape[1] == seg * len(SEGMENTS)
    t_blocks = tab_p[0].shape[0] // tm
    row = lambda i, n: (i, 0)
    fixed = lambda i, n: (0, 0)
    tab = lambda i, n: (i % t_blocks, 0)
    stored = [k for k in kinds if k in SEGMENT_DTYPE]
    has_glu = "glu_a" in kinds
    assert has_glu == ("glu_b" in kinds)
    return pl.pallas_call(
        functools.partial(_inproj_kernel, kinds=tuple(kinds)),
        grid=(m // tm, len(kinds)),
        in_specs=[
            pl.BlockSpec((tm, d), row),
            pl.BlockSpec((md, d), fixed),
            pl.BlockSpec((1, d), fixed),
            pl.BlockSpec((d, seg), lambda i, n: (0, seg0 + n)),
        ] + [pl.BlockSpec((tm, LANES), tab)] * 3 + [pl.BlockSpec((md, LANES), fixed)] * 3,
        out_specs=[pl.BlockSpec((tm, seg), row)] * len(stored) + [pl.BlockSpec((md, seg), fixed)] * len(stored),
        out_shape=[jax.ShapeDtypeStruct((m, seg), SEGMENT_DTYPE[k]) for k in stored]
        + [jax.ShapeDtypeStruct((md, seg), SEGMENT_DTYPE[k]) for k in stored],
        scratch_shapes=[pltpu.VMEM((tm + md, d), BF16)] + ([pltpu.VMEM((tm + md, seg), F32)] if has_glu else []),
        compiler_params=pltpu.CompilerParams(
            dimension_semantics=("arbitrary", "arbitrary"), vmem_limit_bytes=VMEM_LIMIT),
        name="inproj_" + kinds[0],
    )(x2d, xd2d, g_norm.reshape(1, d), w_in_bf, *tab_p, *tab_d)


class _DecodeHost:
    def __init__(self, pt_ref, lam, gs, qd_ref, knd_ref, vnd_ref, sad_ref, ck_hbm, cv_hbm, od_ref,
                 kbuf, vbuf, sems, m_scr, l_scr, acca_scr, accb_scr,
                 *, lam_init, layer, tok0, step, n_steps):
        self.pt_ref, self.lam, self.gs, self.lam_init, self.layer = pt_ref, lam, gs, lam_init, layer
        self.knd_ref, self.vnd_ref, self.sad_ref, self.od_ref = knd_ref, vnd_ref, sad_ref, od_ref
        self.ck_hbm, self.cv_hbm, self.kbuf, self.vbuf, self.sems = ck_hbm, cv_hbm, kbuf, vbuf, sems
        self.state = (m_scr, l_scr, acca_scr, accb_scr)
        self.tok0, self.step, self.n_steps = tok0, step, n_steps
        self.n_tok, self.nh, self.hw = qd_ref.shape
        self.n_parts, self.part = kbuf.shape[0], kbuf.shape[1]
        self.per_tok = self.n_parts * self.part // self.n_tok
        assert self.per_tok * self.n_tok == self.n_parts * self.part
        assert self.per_tok * n_steps == pt_ref.shape[1]
        hw = self.hw
        d_map = lax.broadcasted_iota(jnp.int32, (hw, hw), 0) // HEAD_DIM
        n_map = lax.broadcasted_iota(jnp.int32, (hw, hw), 1) // HEAD_DIM
        self.sel = (d_map == n_map).astype(BF16)
        self.qd = qd_ref[...].astype(F32)

    def _page_copies(self, step_idx, hb, i):
        j = hb * self.part + i
        pid = self.pt_ref[self.tok0 + j // self.per_tok, step_idx * self.per_tok + j % self.per_tok]
        return (pltpu.make_async_copy(self.ck_hbm.at[self.layer, pid], self.kbuf.at[hb, i], self.sems.at[0, hb, i]),
                pltpu.make_async_copy(self.cv_hbm.at[self.layer, pid], self.vbuf.at[hb, i], self.sems.at[1, hb, i]))

    def _start_part(self, step_idx, hb):
        for i in range(self.part):
            for copy in self._page_copies(step_idx, hb, i):
                copy.start()

    def wait_part(self, hb):
        for i in range(self.part):
            for copy in self._page_copies(self.step, hb, i):
                copy.wait()

    def refill_part(self, hb):
        @pl.when(self.step + 1 < self.n_steps)
        def _():
            self._start_part(self.step + 1, hb)

    def begin(self):
        @pl.when(self.step == 0)
        def _():
            for hb in range(self.n_parts):
                self._start_part(0, hb)
            m_scr, l_scr, acca_scr, accb_scr = self.state
            m_scr[...] = jnp.full(m_scr.shape, NEG_INF, F32)
            l_scr[...] = jnp.zeros(l_scr.shape, F32)
            acca_scr[...] = jnp.zeros(acca_scr.shape, F32)
            accb_scr[...] = jnp.zeros(accb_scr.shape, F32)

    @staticmethod
    def _swap_maps(x):
        return pltpu.roll(x, HEAD_DIM, x.ndim - 1)

    @staticmethod
    def _reduce(fn, x):
        ways = DECODE_REDUCE_WAYS if x.shape[0] % DECODE_REDUCE_WAYS == 0 else 1
        return fn(fn(x.reshape(ways, x.shape[0] // ways, *x.shape[1:]), axis=1), axis=0)

    def _scores(self, tok, kp):
        rows = kp.shape[0]
        prod = (kp * self.qd[tok][None]).reshape(rows * self.nh, self.hw)
        return jnp.dot(prod.astype(BF16), self.sel, preferred_element_type=F32)

    def _update(self, t, s, vp):
        m_scr, l_scr, acca_scr, accb_scr = self.state
        nh, hw = self.nh, self.hw
        rows = vp.shape[0]
        m_old = m_scr[t]
        m_new = jnp.maximum(m_old, self._reduce(jnp.max, s.reshape(rows, nh, hw)))
        alpha = jnp.exp2(m_old - m_new)
        p2 = jnp.exp2(s.reshape(rows, nh, hw) - m_new[None]).reshape(rows * nh, hw)
        p = p2.reshape(rows, nh, hw)
        p_sw = self._swap_maps(p2).reshape(rows, nh, hw)
        l_scr[t] = alpha * l_scr[t] + self._reduce(jnp.sum, p)
        acca_scr[t] = alpha * acca_scr[t] + self._reduce(jnp.sum, p * vp)
        accb_scr[t] = self._swap_maps(alpha) * accb_scr[t] + self._reduce(jnp.sum, p_sw * vp)
        m_scr[t] = m_new

    def page_scores(self, hb, i):
        return self._scores((hb * self.part + i) // self.per_tok, self.kbuf[hb, i])

    def page_update(self, hb, i, s):
        self._update(hb * self.part + i, s, self.vbuf[hb, i])

    def finish(self):
        @pl.when(self.step == self.n_steps - 1)
        def _():
            m_scr, l_scr, acca_scr, accb_scr = self.state
            nh, hw, per_tok = self.nh, self.hw, self.per_tok
            lo = lax.broadcasted_iota(jnp.int32, (nh, hw), 1) < HEAD_DIM
            for tok in range(self.n_tok):
                t0 = tok * per_tok
                self._update(t0, self._scores(tok, self.knd_ref[tok][None]), self.vnd_ref[tok][None])
                m_all = m_scr[t0:t0 + per_tok]
                m = jnp.max(m_all, axis=0)
                w = jnp.exp2(m_all - m[None])
                w_sw = self._swap_maps(w.reshape(per_tok * nh, hw)).reshape(per_tok, nh, hw)
                l = jnp.sum(w * l_scr[t0:t0 + per_tok], axis=0)
                acc_a = jnp.sum(w * acca_scr[t0:t0 + per_tok], axis=0)
                acc_b = jnp.sum(w_sw * accb_scr[t0:t0 + per_tok], axis=0)
                na = acc_a / l
                nb = acc_b / self._swap_maps(l)
                o = jnp.where(lo, na, nb) - self.lam * jnp.where(lo, nb, na)
                o = _rms(o, self.gs) * (1.0 - self.lam_init)
                self.od_ref[tok] = o * self.sad_ref[tok].astype(F32)


def _decode_host_specs(n_tok, tok0, nh, hw, page, pages_per_step):
    assert tok0 % n_tok == 0 and pages_per_step % N_HALVES == 0
    tok_spec = pl.BlockSpec((n_tok, nh, hw), lambda *_: (tok0 // n_tok, 0, 0))
    part = pages_per_step // N_HALVES
    scratch = [
        pltpu.VMEM((N_HALVES, part, page, nh, hw), F32), pltpu.VMEM((N_HALVES, part, page, nh, hw), F32),
        pltpu.SemaphoreType.DMA((2, N_HALVES, part)),
    ] + [pltpu.VMEM((pages_per_step, nh, hw), F32)] * 4
    return tok_spec, pl.BlockSpec(memory_space=pl.ANY), scratch


def _attn_kernel(pt_ref, lq1, lk1, lq2, lk2, gs_ref, q_ref, k_ref, v_ref, sa_ref,
                 qd_ref, knd_ref, vnd_ref, sad_ref, ck_hbm, cv_hbm, o_ref, od_ref,
                 kt_scr, vb_scr, kbuf, vbuf, sems, m_scr, l_scr, acca_scr, accb_scr,
                 *, tq, lam_init, layer, tok0, n_steps):
    s_len, hw = k_ref.shape
    step = pl.program_id(0) * pl.num_programs(1) + pl.program_id(1)
    lam = _lam(lq1, lk1, lq2, lk2, lam_init)
    gs = gs_ref[...]
    dec = _DecodeHost(pt_ref, lam, gs, qd_ref, knd_ref, vnd_ref, sad_ref, ck_hbm, cv_hbm, od_ref,
                      kbuf, vbuf, sems, m_scr, l_scr, acca_scr, accb_scr,
                      lam_init=lam_init, layer=layer, tok0=tok0, step=step, n_steps=n_steps)
    dec.begin()

    kt_scr[...] = k_ref[...].T.astype(BF16)
    vb_scr[:, 0:hw] = v_ref[...].astype(BF16)
    vb_scr[:, hw:2 * hw] = jnp.ones((s_len, hw), BF16)
    lane = lax.broadcasted_iota(jnp.int32, (tq, hw), 1)
    q_row = lax.broadcasted_iota(jnp.int32, (2 * tq, tq), 0) % tq
    k_col = lax.broadcasted_iota(jnp.int32, (2 * tq, tq), 1)
    causal = k_col <= q_row

    def attention_tile(i):
        r0, c_diag = i * tq, i * tq
        q = q_ref[r0:r0 + tq, :]
        zero = jnp.zeros_like(q)
        q2 = jnp.concatenate(
            [jnp.where(lane < HEAD_DIM, q, zero), jnp.where(lane >= HEAD_DIM, q, zero)], axis=0)
        s_diag = jnp.dot(q2, kt_scr[:, c_diag:c_diag + tq], preferred_element_type=F32)
        s_diag = jnp.where(causal, s_diag, NEG_INF)
        m = jnp.max(s_diag, axis=-1, keepdims=True)
        if i > 0:
            s_past = jnp.dot(q2, kt_scr[:, 0:c_diag], preferred_element_type=F32)
            m = jnp.maximum(m, jnp.max(s_past, axis=-1, keepdims=True))
        acc = jnp.dot(jnp.exp2(s_diag - m).astype(BF16), vb_scr[c_diag:c_diag + tq, :],
                      preferred_element_type=F32)
        if i > 0:
            acc = acc + jnp.dot(jnp.exp2(s_past - m).astype(BF16), vb_scr[0:c_diag, :],
                                preferred_element_type=F32)
        o = acc[0:tq, 0:hw] / acc[0:tq, hw:] - lam * (acc[tq:, 0:hw] / acc[tq:, hw:])
        o = _rms(o, gs) * (1.0 - lam_init)
        o_ref[r0:r0 + tq, :] = (o * sa_ref[r0:r0 + tq, :].astype(F32)).astype(o_ref.dtype)

    def part_of_step(q_tiles, hb):
        work = sum(i + 1 for i in q_tiles)
        work_done = pages_done = 0
        for i in q_tiles:
            attention_tile(i)
            work_done += i + 1
            while pages_done < work_done * dec.part // work:
                dec.page_update(hb, pages_done, dec.page_scores(hb, pages_done))
                pages_done += 1

    n_q = s_len // tq
    tiles_upto = lambda n: n * (n + 1) // 2
    split = min(range(n_q + 1), key=lambda n: abs(2 * tiles_upto(n) - tiles_upto(n_q)))
    for hb, q_tiles in enumerate((range(0, split), range(split, n_q))):
        dec.wait_part(hb)
        part_of_step(q_tiles, hb)
        dec.refill_part(hb)
    dec.finish()


def _attention(q, k, v, sa, dec_ops, caches, page_table, layer, tok0, n_tok, lams, g_subln, tq, lam_init):
    b, s, aw = q.shape
    _, nh, hw = dec_ops[0].shape
    n_pages, page = page_table.shape[1], caches[0].shape[2]
    n_steps = b * nh
    assert aw == nh * hw and (n_tok * n_pages) % n_steps == 0
    pages_per_step = n_tok * n_pages // n_steps
    head = lambda bi, h, pt: (bi, 0, h)
    small = lambda bi, h, pt: (0, 0)
    head_spec = pl.BlockSpec((None, s, hw), head)
    tok_spec, hbm_spec, dec_scratch = _decode_host_specs(n_tok, tok0, nh, hw, page, pages_per_step)
    grid_spec = pltpu.PrefetchScalarGridSpec(
        num_scalar_prefetch=1,
        grid=(b, nh),
        in_specs=[pl.BlockSpec((1, HEAD_DIM), small)] * 4 + [pl.BlockSpec((1, hw), small)]
        + [head_spec] * 4 + [tok_spec] * 4 + [hbm_spec] * 2,
        out_specs=[head_spec, pl.BlockSpec((n_tok, nh, hw), lambda bi, h, pt: (0, 0, 0))],
        scratch_shapes=[pltpu.VMEM((hw, s), BF16), pltpu.VMEM((s, 2 * hw), BF16)] + dec_scratch,
    )
    return pl.pallas_call(
        functools.partial(_attn_kernel, tq=tq, lam_init=lam_init, layer=layer, tok0=tok0, n_steps=n_steps),
        grid_spec=grid_spec,
        out_shape=[jax.ShapeDtypeStruct((b, s, aw), BF16), jax.ShapeDtypeStruct((n_tok, nh, hw), F32)],
        compiler_params=pltpu.CompilerParams(
            dimension_semantics=("arbitrary", "arbitrary"), vmem_limit_bytes=VMEM_LIMIT),
        name="attention",
    )(page_table, *lams, g_subln.reshape(1, hw), q, k, v, sa, *dec_ops, *caches)


def _conv_kernel(gc_ref, gh_ref, wdw_ref, bdw_ref, gcl_ref, bcl_ref, wpw_ref, sc_ref, o_ref,
                 sh_scr, c_scr, *, ts, tc, halo, kw, rc):
    i = pl.program_id(1)
    c = pl.program_id(2)
    nc = c_scr.shape[0]
    off = halo - (kw - 1)
    n_copy = halo + ts - SUBLANES
    sh_scr[0, 0:halo, :] = jnp.where(i == 0, 0.0, gh_ref[...])
    sh_scr[0, halo:halo + ts, :] = gc_ref[...]
    for ph in range(1, SUBLANES):
        sh_scr[ph, 0:n_copy, :] = sh_scr[0, ph:ph + n_copy, :]
    for r in range(ts // rc):
        acc = jnp.broadcast_to(bdw_ref[...], (rc, tc))
        for j in range(kw):
            ph, base = (off + j) % SUBLANES, (off + j) // SUBLANES * SUBLANES
            acc = acc + wdw_ref[j:j + 1, :] * sh_scr[ph, r * rc + base:r * rc + base + rc, :]
        c_scr[c, r * rc:(r + 1) * rc, :] = acc

    @pl.when(c == nc - 1)
    def _norm_pw():
        width = nc * tc
        cs = [c_scr[k] for k in range(nc)]
        mu = sum(jnp.sum(x, axis=-1, keepdims=True) for x in cs) / width
        var = sum(jnp.sum(jnp.square(x - mu), axis=-1, keepdims=True) for x in cs) / width
        inv = lax.rsqrt(var + EPS)
        acc = jnp.zeros(o_ref.shape, F32)
        for k in range(nc):
            y = (cs[k] - mu) * inv * gcl_ref[:, k * tc:(k + 1) * tc] + bcl_ref[:, k * tc:(k + 1) * tc]
            acc = acc + jnp.dot(_silu(y).astype(BF16), wpw_ref[k * tc:(k + 1) * tc, :],
                                preferred_element_type=F32)
        o_ref[...] = (acc * sc_ref[...].astype(F32)).astype(o_ref.dtype)


def _prompt_conv(g, sc, w_dw, b_dw, g_cln, b_cln, w_pw_bf, ts, tc):
    b, s, cw = g.shape
    kw = w_dw.shape[0]
    halo = -(-(kw - 1) // SUBLANES) * SUBLANES
    assert ts % halo == 0 and s % ts == 0 and cw % tc == 0
    hb = ts // halo
    vec = lambda a: a.reshape(1, cw)
    full = lambda bi, i, c: (0, 0)
    return pl.pallas_call(
        functools.partial(_conv_kernel, ts=ts, tc=tc, halo=halo, kw=kw, rc=64),
        grid=(b, s // ts, cw // tc),
        in_specs=[
            pl.BlockSpec((None, ts, tc), lambda bi, i, c: (bi, i, c)),
            pl.BlockSpec((None, halo, tc), lambda bi, i, c: (bi, jnp.maximum(i * hb - 1, 0), c)),
            pl.BlockSpec((kw, tc), lambda bi, i, c: (0, c)),
            pl.BlockSpec((1, tc), lambda bi, i, c: (0, c)),
            pl.BlockSpec((1, cw), full),
            pl.BlockSpec((1, cw), full),
            pl.BlockSpec((cw, cw), full),
            pl.BlockSpec((None, ts, cw), lambda bi, i, c: (bi, i, 0)),
        ],
        out_specs=pl.BlockSpec((None, ts, cw), lambda bi, i, c: (bi, i, 0)),
        out_shape=jax.ShapeDtypeStruct((b, s, cw), BF16),
        scratch_shapes=[pltpu.VMEM((SUBLANES, halo + ts, tc), F32),
                        pltpu.VMEM((cw // tc, ts, tc), F32)],
        compiler_params=pltpu.CompilerParams(
            dimension_semantics=("parallel", "parallel", "arbitrary"), vmem_limit_bytes=VMEM_LIMIT),
        name="prompt_conv",
    )(g, g, w_dw, vec(b_dw), vec(g_cln), vec(b_cln), w_pw_bf, sc)


def _decode_conv_kernel(hist_ref, wdw_ref, bdw_ref, gcl_ref, bcl_ref, wpw_ref, sc_ref, o_ref):
    kw = hist_ref.shape[0]
    acc = jnp.broadcast_to(bdw_ref[...], o_ref.shape)
    for j in range(kw):
        acc = acc + wdw_ref[j:j + 1, :] * hist_ref[j]
    mu = jnp.mean(acc, axis=-1, keepdims=True)
    var = jnp.mean(jnp.square(acc - mu), axis=-1, keepdims=True)
    y = (acc - mu) * lax.rsqrt(var + EPS) * gcl_ref[...] + bcl_ref[...]
    c = jnp.dot(_silu(y).astype(BF16), wpw_ref[...], preferred_element_type=F32)
    o_ref[...] = (c * sc_ref[...].astype(F32)).astype(o_ref.dtype)


def _decode_conv(hist_t, sc, w_dw, b_dw, g_cln, b_cln, w_pw_bf):
    kw, db, cw = hist_t.shape
    vec = lambda a: a.reshape(1, cw)
    return pl.pallas_call(
        _decode_conv_kernel,
        out_shape=jax.ShapeDtypeStruct((db, cw), BF16),
        compiler_params=pltpu.CompilerParams(vmem_limit_bytes=VMEM_LIMIT),
        name="decode_conv",
    )(hist_t, w_dw, vec(b_dw), vec(g_cln), vec(b_cln), w_pw_bf, sc)


def _out_rows(rows, x_ref, oa_ref, oc_ref, wout_ref):
    aw = oa_ref.shape[1]
    h = x_ref[rows, :]
    h = h + jnp.dot(oa_ref[rows, :], wout_ref[0:aw, :], preferred_element_type=F32)
    return h + jnp.dot(oc_ref[rows, :], wout_ref[aw:, :], preferred_element_type=F32)


def _ple_rows(rows, h, p_ref, gple_ref, wpg_ref, wple_ref, gfin_ref, y_ref, final_norm):
    r = _rms(h, gple_ref[...]).astype(BF16)
    gate = _sigmoid(jnp.dot(r, wpg_ref[...], preferred_element_type=F32))
    e = jnp.dot(p_ref[rows, :].astype(BF16), wple_ref[...], preferred_element_type=F32)
    h = h + gate * e
    y_ref[rows, :] = _rms(h, gfin_ref[...]) if final_norm else h


def _out_kernel(x_ref, oa_ref, oc_ref, p_ref, wout_ref, gple_ref, wpg_ref, wple_ref, gfin_ref, y_ref,
                *, final_norm):
    tm = oa_ref.shape[0]
    sub = min(tm, OUT_SUBTILE)
    for r0 in range(0, tm, sub):
        rows = slice(r0, r0 + sub)
        h = _out_rows(rows, x_ref, oa_ref, oc_ref, wout_ref)
        _ple_rows(rows, h, p_ref, gple_ref, wpg_ref, wple_ref, gfin_ref, y_ref, final_norm)


def _out_decode_kernel(pt_ref, lq1, lk1, lq2, lk2, gs_ref,
                       x_ref, oa_ref, oc_ref, p_ref, wout_ref, gple_ref, wpg_ref, wple_ref, gfin_ref,
                       qd_ref, knd_ref, vnd_ref, sad_ref, ck_hbm, cv_hbm, y_ref, od_ref,
                       kbuf, vbuf, sems, m_scr, l_scr, acca_scr, accb_scr,
                       *, final_norm, lam_init, layer, tok0, n_steps):
    rows = slice(0, oa_ref.shape[0])
    dec = _DecodeHost(pt_ref, _lam(lq1, lk1, lq2, lk2, lam_init), gs_ref[...],
                      qd_ref, knd_ref, vnd_ref, sad_ref, ck_hbm, cv_hbm, od_ref,
                      kbuf, vbuf, sems, m_scr, l_scr, acca_scr, accb_scr,
                      lam_init=lam_init, layer=layer, tok0=tok0, step=pl.program_id(0), n_steps=n_steps)
    dec.begin()
    dec.wait_part(0)
    scores = [dec.page_scores(0, i) for i in range(dec.part)]
    h = _out_rows(rows, x_ref, oa_ref, oc_ref, wout_ref)
    for i in range(dec.part):
        dec.page_update(0, i, scores[i])
    dec.refill_part(0)
    dec.wait_part(1)
    scores = [dec.page_scores(1, i) for i in range(dec.part)]
    _ple_rows(rows, h, p_ref, gple_ref, wpg_ref, wple_ref, gfin_ref, y_ref, final_norm)
    for i in range(dec.part):
        dec.page_update(1, i, scores[i])
    dec.refill_part(1)
    dec.finish()


def _outproj(x2d, oa, oc, p2d, w_out_bf, g_ple, w_pg_bf, w_ple_bf, g_final, tm, final_norm, decode=None):
    m, d = x2d.shape
    aw, cw, pd = oa.shape[1], oc.shape[1], p2d.shape[1]
    row = lambda i, *_: (i, 0)
    full = lambda i, *_: (0, 0)
    once = pl.Buffered(1)
    in_specs = [
        pl.BlockSpec((tm, d), row),
        pl.BlockSpec((tm, aw), row),
        pl.BlockSpec((tm, cw), row),
        pl.BlockSpec((tm, pd), row),
        pl.BlockSpec((aw + cw, d), full, pipeline_mode=once),
        pl.BlockSpec((1, d), full),
        pl.BlockSpec((d, d), full, pipeline_mode=once),
        pl.BlockSpec((pd, d), full, pipeline_mode=once),
        pl.BlockSpec((1, d), full),
    ]
    operands = (x2d, oa, oc, p2d, w_out_bf, g_ple.reshape(1, d), w_pg_bf, w_ple_bf, g_final.reshape(1, d))
    y_spec, y_shape = pl.BlockSpec((tm, d), row), jax.ShapeDtypeStruct((m, d), F32)
    if decode is None:
        return pl.pallas_call(
            functools.partial(_out_kernel, final_norm=final_norm),
            grid=(m // tm,), in_specs=in_specs, out_specs=y_spec, out_shape=y_shape,
            compiler_params=pltpu.CompilerParams(
                dimension_semantics=("parallel",), vmem_limit_bytes=VMEM_LIMIT),
            name="outproj",
        )(*operands)
    dec_ops, caches, page_table, layer, tok0, n_tok, lams, g_subln, lam_init = decode
    _, nh, hw = dec_ops[0].shape
    n_pages, page = page_table.shape[1], caches[0].shape[2]
    n_steps = m // tm
    assert (n_tok * n_pages) % n_steps == 0
    tok_spec, hbm_spec, dec_scratch = _decode_host_specs(n_tok, tok0, nh, hw, page, n_tok * n_pages // n_steps)
    grid_spec = pltpu.PrefetchScalarGridSpec(
        num_scalar_prefetch=1,
        grid=(n_steps,),
        in_specs=[pl.BlockSpec((1, HEAD_DIM), full)] * 4 + [pl.BlockSpec((1, hw), full)] + in_specs
        + [tok_spec] * 4 + [hbm_spec] * 2,
        out_specs=[y_spec, pl.BlockSpec((n_tok, nh, hw), lambda i, pt: (0, 0, 0))],
        scratch_shapes=dec_scratch,
    )
    return pl.pallas_call(
        functools.partial(_out_decode_kernel, final_norm=final_norm, lam_init=lam_init, layer=layer,
                          tok0=tok0, n_steps=n_steps),
        grid_spec=grid_spec,
        out_shape=[y_shape, jax.ShapeDtypeStruct((n_tok, nh, hw), F32)],
        compiler_params=pltpu.CompilerParams(
            dimension_semantics=("arbitrary",), vmem_limit_bytes=VMEM_LIMIT),
        name="outproj_decode",
    )(page_table, *lams, g_subln.reshape(1, hw), *operands, *dec_ops, *caches)


def kernel(x_prompt, x_sample, p_prompt, p_sample, cache_k, cache_v, state_conv, page_table, w_in, g_norm, lam_q1, lam_k1, lam_q2, lam_k2, g_subln, w_dw, b_dw, g_cln, b_cln, w_pw, w_out, g_ple, w_pg, w_ple, g_final):
    b, s, d = x_prompt.shape
    db, ds, _ = x_sample.shape
    depth = w_in.shape[0]
    nh, hw = cache_k.shape[3], cache_k.shape[4]
    aw = nh * hw
    cw = w_dw.shape[2]
    kw = w_dw.shape[1]
    past = page_table.shape[1] * cache_k.shape[2]
    assert ds == 1 and aw == cw and hw == 2 * HEAD_DIM

    tab_p = _rope_tables(jnp.arange(s, dtype=jnp.int32))
    tab_s = _rope_tables(jnp.full((db * ds,), past, dtype=jnp.int32))

    h_p = x_prompt.reshape(b * s, d)
    h_s = x_sample.reshape(db * ds, d)
    outs = {k: [] for k in ("kp", "vp", "cp", "ks", "vs", "cs")}
    for i in range(depth):
        lam_init = _lam_init(i)
        last = i == depth - 1
        lams = tuple(a[i].reshape(1, HEAD_DIM) for a in (lam_q1, lam_k1, lam_q2, lam_k2))
        w_in_bf, w_pw_bf = w_in[i].astype(BF16), w_pw[i].astype(BF16)
        w_out_bf, w_pg_bf, w_ple_bf = w_out[i].astype(BF16), w_pg[i].astype(BF16), w_ple[i].astype(BF16)

        q, k, v, qd, kd, vd = _inproj(h_p, h_s, g_norm[i], w_in_bf, tab_p, tab_s, tm=512, seg=aw,
                                      kinds=("q", "k", "v"))
        sa, g, sc, sad, gd, scd = _inproj(h_p, h_s, g_norm[i], w_in_bf, tab_p, tab_s, tm=1024, seg=aw,
                                          kinds=("gate_a", "glu_a", "glu_b", "gate_c"))
        as3 = lambda a: a.reshape(b, s, -1)
        dec_ops = tuple(a.reshape(db, nh, hw) for a in (qd, kd, vd, sad))
        caches = (cache_k, cache_v)
        n_att = 3 * db // 4

        oa, oad_a = _attention(as3(q), as3(k), as3(v), as3(sa), dec_ops, caches, page_table, i, 0, n_att,
                               lams, g_subln[i], tq=256, lam_init=lam_init)
        oc = _prompt_conv(as3(g), as3(sc), w_dw[i], b_dw[i], g_cln[i], b_cln[i], w_pw_bf, ts=256, tc=256)
        h_p, oad_o = _outproj(h_p, oa.reshape(b * s, aw), oc.reshape(b * s, cw), p_prompt[i].reshape(b * s, -1),
                              w_out_bf, g_ple[i], w_pg_bf, w_ple_bf, g_final, tm=256, final_norm=last,
                              decode=(dec_ops, caches, page_table, i, n_att, db - n_att, lams, g_subln[i], lam_init))
        oad = jnp.concatenate([oad_a, oad_o], axis=0)
        outs["kp"].append(k.reshape(b, s, nh, hw))
        outs["vp"].append(v.reshape(b, s, nh, hw))
        outs["cp"].append(as3(g)[:, s - (kw - 1):])

        hist = jnp.concatenate([state_conv[i], gd.reshape(db, ds, cw)], axis=1)
        ocd = _decode_conv(hist.transpose(1, 0, 2), scd, w_dw[i], b_dw[i], g_cln[i], b_cln[i], w_pw_bf)
        h_s = _outproj(h_s, oad.reshape(db, aw).astype(BF16), ocd, p_sample[i].reshape(db * ds, -1),
                       w_out_bf, g_ple[i], w_pg_bf, w_ple_bf, g_final, tm=db * ds, final_norm=last)
        outs["ks"].append(kd.reshape(db, ds, nh, hw))
        outs["vs"].append(vd.reshape(db, ds, nh, hw))
        outs["cs"].append(hist[:, 1:])

    st = lambda key: jnp.stack(outs[key])
    return (h_p.reshape(b, s, d), h_s.reshape(db, ds, d),
            st("kp"), st("vp"), st("cp"), st("ks"), st("vs"), st("cs"))
```

```python
import functools
import math

import jax
import jax.numpy as jnp
from jax import lax
from jax.experimental import pallas as pl
from jax.experimental.pallas import tpu as pltpu

F32 = jnp.float32
BF16 = jnp.bfloat16

HEAD_DIM = 64
ROT_DIM = HEAD_DIM // 4
HALF_ROT = ROT_DIM // 2
ROPE_THETA = 500000.0
EPS = 1e-6
NEG_INF = -1e30
Q_SCALE = HEAD_DIM ** -0.5 * math.log2(math.e)

LANES = 128
SUBLANES = 8
VMEM_LIMIT = 56 * 1024 * 1024
MXU_DIM = 256
INPROJ_CHUNK = MXU_DIM
OUT_SUBTILE = 256
N_HALVES = 2
DECODE_REDUCE_WAYS = 4


def _sigmoid(x):
    return 1.0 / (1.0 + jnp.exp(-x))


def _silu(x):
    return x * _sigmoid(x)


def _rms(x, g):
    return x * lax.rsqrt(jnp.mean(x * x, axis=-1, keepdims=True) + EPS) * g


def _lam_init(layer_idx):
    return 0.8 - 0.6 * math.exp(-0.3 * layer_idx)


def _lam(lq1, lk1, lq2, lk2, lam_init):
    a = jnp.sum(lq1[...] * lk1[...], keepdims=True)
    b = jnp.sum(lq2[...] * lk2[...], keepdims=True)
    return jnp.exp(a) - jnp.exp(b) + lam_init


def _rope_tables(pos):
    inv = ROPE_THETA ** (-jnp.arange(0, ROT_DIM, 2, dtype=F32) / ROT_DIM)
    ang = pos.astype(F32)[:, None] * inv[None, :]
    d = jnp.arange(2 * HEAD_DIM) % HEAD_DIM
    f = d % HALF_ROT
    cos_f = jnp.cos(ang)[:, f]
    sin_f = jnp.sin(ang)[:, f]
    lo = (d < HALF_ROT)[None, :]
    hi = ((d >= HALF_ROT) & (d < ROT_DIM))[None, :]
    cos_t = jnp.where(lo | hi, cos_f, 1.0)
    s_next = jnp.where(lo, -sin_f, 0.0)
    s_prev = jnp.where(hi, sin_f, 0.0)
    return cos_t, s_next, s_prev


SEGMENTS = ("q", "k", "v", "gate_a", "glu_a", "glu_b", "gate_c")
SEGMENT_DTYPE = {"q": BF16, "k": F32, "v": F32, "gate_a": BF16, "glu_b": F32, "gate_c": BF16}


def _inproj_kernel(*refs, kinds):
    (x_ref, xd_ref, gn_ref, w_ref, cos_ref, snext_ref, sprev_ref, cosd_ref, snextd_ref, sprevd_ref), refs = refs[:10], refs[10:]
    stored = [k for k in kinds if k in SEGMENT_DTYPE]
    out_p = dict(zip(stored, refs[:len(stored)]))
    out_d = dict(zip(stored, refs[len(stored):2 * len(stored)]))
    u_scr, *glu_scr = refs[2 * len(stored):]
    n = pl.program_id(1)
    tm, seg = x_ref.shape[0], w_ref.shape[1]

    def segment(epilogue):
        u = u_scr[...]
        for c0 in range(0, seg, INPROJ_CHUNK):
            cols = slice(c0, c0 + INPROJ_CHUNK)
            z = jnp.dot(u, w_ref[:, cols], preferred_element_type=F32)
            epilogue(z, cols)

    def rope(z, scale):
        tables = [jnp.concatenate([t[...], td[...]], axis=0)
                  for t, td in ((cos_ref, cosd_ref), (snext_ref, snextd_ref), (sprev_ref, sprevd_ref))]
        cos_t, s_next, s_prev = tables
        heads = []
        for h in range(z.shape[1] // LANES):
            zh = z[:, h * LANES:(h + 1) * LANES]
            nxt = pltpu.roll(zh, LANES - HALF_ROT, 1)
            prv = pltpu.roll(zh, HALF_ROT, 1)
            heads.append((zh * cos_t + nxt * s_next + prv * s_prev) * scale)
        return jnp.concatenate(heads, axis=1)

    def store(kind, fn):
        def epilogue(z, cols):
            r = fn(z, cols)
            out_p[kind][:, cols] = r[0:tm].astype(out_p[kind].dtype)
            out_d[kind][:, cols] = r[tm:].astype(out_d[kind].dtype)
        return epilogue

    epilogues = {
        "q": store("q", lambda z, cols: rope(z, Q_SCALE)),
        "k": store("k", lambda z, cols: rope(z, 1.0)),
        "v": store("v", lambda z, cols: z),
        "gate_a": store("gate_a", lambda z, cols: _silu(z)),
        "gate_c": store("gate_c", lambda z, cols: _silu(z)),
        "glu_b": store("glu_b", lambda z, cols: glu_scr[0][:, cols] * _sigmoid(z)),
    }

    def keep_glu_a(z, cols):
        glu_scr[0][:, cols] = z

    for idx, kind in enumerate(kinds):
        @pl.when(n == idx)
        def _(idx=idx, kind=kind):
            if idx == 0:
                u_scr[0:tm, :] = _rms(x_ref[...], gn_ref[...]).astype(BF16)
                u_scr[tm:, :] = _rms(xd_ref[...], gn_ref[...]).astype(BF16)
            segment(keep_glu_a if kind == "glu_a" else epilogues[kind])


def _inproj(x2d, xd2d, g_norm, w_in_bf, tab_p, tab_d, tm, seg, kinds):
    m, d = x2d.shape
    md = xd2d.shape[0]
    seg0 = SEGMENTS.index(kinds[0])
    assert SEGMENTS[seg0:seg0 + len(kinds)] == tuple(kinds) and m % tm == 0 and w_in_bf.shape[1] == seg * len(SEGMENTS)
    t_blocks = tab_p[0].shape[0] // tm
    row = lambda i, n: (i, 0)
    fixed = lambda i, n: (0, 0)
    tab = lambda i, n: (i % t_blocks, 0)
    stored = [k for k in kinds if k in SEGMENT_DTYPE]
    has_glu = "glu_a" in kinds
    assert has_glu == ("glu_b" in kinds)
    return pl.pallas_call(
        functools.partial(_inproj_kernel, kinds=tuple(kinds)),
        grid=(m // tm, len(kinds)),
        in_specs=[
            pl.BlockSpec((tm, d), row),
            pl.BlockSpec((md, d), fixed),
            pl.BlockSpec((1, d), fixed),
            pl.BlockSpec((d, seg), lambda i, n: (0, seg0 + n)),
        ] + [pl.BlockSpec((tm, LANES), tab)] * 3 + [pl.BlockSpec((md, LANES), fixed)] * 3,
        out_specs=[pl.BlockSpec((tm, seg), row)] * len(stored) + [pl.BlockSpec((md, seg), fixed)] * len(stored),
        out_shape=[jax.ShapeDtypeStruct((m, seg), SEGMENT_DTYPE[k]) for k in stored]
        + [jax.ShapeDtypeStruct((md, seg), SEGMENT_DTYPE[k]) for k in stored],
        scratch_shapes=[pltpu.VMEM((tm + md, d), BF16)] + ([pltpu.VMEM((tm + md, seg), F32)] if has_glu else []),
        compiler_params=pltpu.CompilerParams(
            dimension_semantics=("arbitrary", "arbitrary"), vmem_limit_bytes=VMEM_LIMIT),
        name="inproj_" + kinds[0],
    )(x2d, xd2d, g_norm.reshape(1, d), w_in_bf, *tab_p, *tab_d)


class _DecodeHost:
    def __init__(self, pt_ref, lam, gs, qd_ref, knd_ref, vnd_ref, sad_ref, ck_hbm, cv_hbm, od_ref,
                 kbuf, vbuf, sems, m_scr, l_scr, acca_scr, accb_scr,
                 *, lam_init, layer, tok0, step, n_steps):
        self.pt_ref, self.lam, self.gs, self.lam_init, self.layer = pt_ref, lam, gs, lam_init, layer
        self.knd_ref, self.vnd_ref, self.sad_ref, self.od_ref = knd_ref, vnd_ref, sad_ref, od_ref
        self.ck_hbm, self.cv_hbm, self.kbuf, self.vbuf, self.sems = ck_hbm, cv_hbm, kbuf, vbuf, sems
        self.state = (m_scr, l_scr, acca_scr, accb_scr)
        self.tok0, self.step, self.n_steps = tok0, step, n_steps
        self.n_tok, self.nh, self.hw = qd_ref.shape
        self.n_parts, self.part = kbuf.shape[0], kbuf.shape[1]
        self.per_tok = self.n_parts * self.part // self.n_tok
        assert self.per_tok * self.n_tok == self.n_parts * self.part
        assert self.per_tok * n_steps == pt_ref.shape[1]
        hw = self.hw
        d_map = lax.broadcasted_iota(jnp.int32, (hw, hw), 0) // HEAD_DIM
        n_map = lax.broadcasted_iota(jnp.int32, (hw, hw), 1) // HEAD_DIM
        self.sel = (d_map == n_map).astype(BF16)
        self.qd = qd_ref[...].astype(F32)

    def _page_copies(self, step_idx, hb, i):
        j = hb * self.part + i
        pid = self.pt_ref[self.tok0 + j // self.per_tok, step_idx * self.per_tok + j % self.per_tok]
        return (pltpu.make_async_copy(self.ck_hbm.at[self.layer, pid], self.kbuf.at[hb, i], self.sems.at[0, hb, i]),
                pltpu.make_async_copy(self.cv_hbm.at[self.layer, pid], self.vbuf.at[hb, i], self.sems.at[1, hb, i]))

    def _start_part(self, step_idx, hb):
        for i in range(self.part):
            for copy in self._page_copies(step_idx, hb, i):
                copy.start()

    def wait_part(self, hb):
        for i in range(self.part):
            for copy in self._page_copies(self.step, hb, i):
                copy.wait()

    def refill_part(self, hb):
        @pl.when(self.step + 1 < self.n_steps)
        def _():
            self._start_part(self.step + 1, hb)

    def begin(self):
        @pl.when(self.step == 0)
        def _():
            for hb in range(self.n_parts):
                self._start_part(0, hb)
            m_scr, l_scr, acca_scr, accb_scr = self.state
            m_scr[...] = jnp.full(m_scr.shape, NEG_INF, F32)
            l_scr[...] = jnp.zeros(l_scr.shape, F32)
            acca_scr[...] = jnp.zeros(acca_scr.shape, F32)
            accb_scr[...] = jnp.zeros(accb_scr.shape, F32)

    @staticmethod
    def _swap_maps(x):
        return pltpu.roll(x, HEAD_DIM, x.ndim - 1)

    @staticmethod
    def _reduce(fn, x):
        ways = DECODE_REDUCE_WAYS if x.shape[0] % DECODE_REDUCE_WAYS == 0 else 1
        return fn(fn(x.reshape(ways, x.shape[0] // ways, *x.shape[1:]), axis=1), axis=0)

    def _scores(self, tok, kp):
        rows = kp.shape[0]
        prod = (kp * self.qd[tok][None]).reshape(rows * self.nh, self.hw)
        return jnp.dot(prod.astype(BF16), self.sel, preferred_element_type=F32)

    def _update(self, t, s, vp):
        m_scr, l_scr, acca_scr, accb_scr = self.state
        nh, hw = self.nh, self.hw
        rows = vp.shape[0]
        m_old = m_scr[t]
        m_new = jnp.maximum(m_old, self._reduce(jnp.max, s.reshape(rows, nh, hw)))
        alpha = jnp.exp2(m_old - m_new)
        p2 = jnp.exp2(s.reshape(rows, nh, hw) - m_new[None]).reshape(rows * nh, hw)
        p = p2.reshape(rows, nh, hw)
        p_sw = self._swap_maps(p2).reshape(rows, nh, hw)
        l_scr[t] = alpha * l_scr[t] + self._reduce(jnp.sum, p)
        acca_scr[t] = alpha * acca_scr[t] + self._reduce(jnp.sum, p * vp)
        accb_scr[t] = self._swap_maps(alpha) * accb_scr[t] + self._reduce(jnp.sum, p_sw * vp)
        m_scr[t] = m_new

    def page_scores(self, hb, i):
        return self._scores((hb * self.part + i) // self.per_tok, self.kbuf[hb, i])

    def page_update(self, hb, i, s):
        self._update(hb * self.part + i, s, self.vbuf[hb, i])

    def finish(self):
        @pl.when(self.step == self.n_steps - 1)
        def _():
            m_scr, l_scr, acca_scr, accb_scr = self.state
            nh, hw, per_tok = self.nh, self.hw, self.per_tok
            lo = lax.broadcasted_iota(jnp.int32, (nh, hw), 1) < HEAD_DIM
            for tok in range(self.n_tok):
                t0 = tok * per_tok
                self._update(t0, self._scores(tok, self.knd_ref[tok][None]), self.vnd_ref[tok][None])
                m_all = m_scr[t0:t0 + per_tok]
                m = jnp.max(m_all, axis=0)
                w = jnp.exp2(m_all - m[None])
                w_sw = self._swap_maps(w.reshape(per_tok * nh, hw)).reshape(per_tok, nh, hw)
                l = jnp.sum(w * l_scr[t0:t0 + per_tok], axis=0)
                acc_a = jnp.sum(w * acca_scr[t0:t0 + per_tok], axis=0)
                acc_b = jnp.sum(w_sw * accb_scr[t0:t0 + per_tok], axis=0)
                na = acc_a / l
                nb = acc_b / self._swap_maps(l)
                o = jnp.where(lo, na, nb) - self.lam * jnp.where(lo, nb, na)
                o = _rms(o, self.gs) * (1.0 - self.lam_init)
                self.od_ref[tok] = o * self.sad_ref[tok].astype(F32)


def _decode_host_specs(n_tok, tok0, nh, hw, page, pages_per_step):
    assert tok0 % n_tok == 0 and pages_per_step % N_HALVES == 0
    tok_spec = pl.BlockSpec((n_tok, nh, hw), lambda *_: (tok0 // n_tok, 0, 0))
    part = pages_per_step // N_HALVES
    scratch = [
        pltpu.VMEM((N_HALVES, part, page, nh, hw), F32), pltpu.VMEM((N_HALVES, part, page, nh, hw), F32),
        pltpu.SemaphoreType.DMA((2, N_HALVES, part)),
    ] + [pltpu.VMEM((pages_per_step, nh, hw), F32)] * 4
    return tok_spec, pl.BlockSpec(memory_space=pl.ANY), scratch


def _attn_kernel(pt_ref, lq1, lk1, lq2, lk2, gs_ref, q_ref, k_ref, v_ref, sa_ref,
                 qd_ref, knd_ref, vnd_ref, sad_ref, ck_hbm, cv_hbm, o_ref, od_ref,
                 kt_scr, vb_scr, kbuf, vbuf, sems, m_scr, l_scr, acca_scr, accb_scr,
                 *, tq, lam_init, layer, tok0, n_steps):
    s_len, hw = k_ref.shape
    step = pl.program_id(0) * pl.num_programs(1) + pl.program_id(1)
    lam = _lam(lq1, lk1, lq2, lk2, lam_init)
    gs = gs_ref[...]
    dec = _DecodeHost(pt_ref, lam, gs, qd_ref, knd_ref, vnd_ref, sad_ref, ck_hbm, cv_hbm, od_ref,
                      kbuf, vbuf, sems, m_scr, l_scr, acca_scr, accb_scr,
                      lam_init=lam_init, layer=layer, tok0=tok0, step=step, n_steps=n_steps)
    dec.begin()

    kt_scr[...] = k_ref[...].T.astype(BF16)
    vb_scr[:, 0:hw] = v_ref[...].astype(BF16)
    vb_scr[:, hw:2 * hw] = jnp.ones((s_len, hw), BF16)
    lane = lax.broadcasted_iota(jnp.int32, (tq, hw), 1)
    q_row = lax.broadcasted_iota(jnp.int32, (2 * tq, tq), 0) % tq
    k_col = lax.broadcasted_iota(jnp.int32, (2 * tq, tq), 1)
    causal = k_col <= q_row

    def attention_tile(i):
        r0, c_diag = i * tq, i * tq
        q = q_ref[r0:r0 + tq, :]
        zero = jnp.zeros_like(q)
        q2 = jnp.concatenate(
            [jnp.where(lane < HEAD_DIM, q, zero), jnp.where(lane >= HEAD_DIM, q, zero)], axis=0)
        s_diag = jnp.dot(q2, kt_scr[:, c_diag:c_diag + tq], preferred_element_type=F32)
        s_diag = jnp.where(causal, s_diag, NEG_INF)
        m = jnp.max(s_diag, axis=-1, keepdims=True)
        if i > 0:
            s_past = jnp.dot(q2, kt_scr[:, 0:c_diag], preferred_element_type=F32)
            m = jnp.maximum(m, jnp.max(s_past, axis=-1, keepdims=True))
        acc = jnp.dot(jnp.exp2(s_diag - m).astype(BF16), vb_scr[c_diag:c_diag + tq, :],
                      preferred_element_type=F32)
        if i > 0:
            acc = acc + jnp.dot(jnp.exp2(s_past - m).astype(BF16), vb_scr[0:c_diag, :],
                                preferred_element_type=F32)
        o = acc[0:tq, 0:hw] / acc[0:tq, hw:] - lam * (acc[tq:, 0:hw] / acc[tq:, hw:])
        o = _rms(o, gs) * (1.0 - lam_init)
        o_ref[r0:r0 + tq, :] = (o * sa_ref[r0:r0 + tq, :].astype(F32)).astype(o_ref.dtype)

    def part_of_step(q_tiles, hb):
        work = sum(i + 1 for i in q_tiles)
        work_done = pages_done = 0
        for i in q_tiles:
            attention_tile(i)
            work_done += i + 1
            while pages_done < work_done * dec.part // work:
                dec.page_update(hb, pages_done, dec.page_scores(hb, pages_done))
                pages_done += 1

    n_q = s_len // tq
    tiles_upto = lambda n: n * (n + 1) // 2
    split = min(range(n_q + 1), key=lambda n: abs(2 * tiles_upto(n) - tiles_upto(n_q)))
    for hb, q_tiles in enumerate((range(0, split), range(split, n_q))):
        dec.wait_part(hb)
        part_of_step(q_tiles, hb)
        dec.refill_part(hb)
    dec.finish()


def _attention(q, k, v, sa, dec_ops, caches, page_table, layer, tok0, n_tok, lams, g_subln, tq, lam_init):
    b, s, aw = q.shape
    _, nh, hw = dec_ops[0].shape
    n_pages, page = page_table.shape[1], caches[0].shape[2]
    n_steps = b * nh
    assert aw == nh * hw and (n_tok * n_pages) % n_steps == 0
    pages_per_step = n_tok * n_pages // n_steps
    head = lambda bi, h, pt: (bi, 0, h)
    small = lambda bi, h, pt: (0, 0)
    head_spec = pl.BlockSpec((None, s, hw), head)
    tok_spec, hbm_spec, dec_scratch = _decode_host_specs(n_tok, tok0, nh, hw, page, pages_per_step)
    grid_spec = pltpu.PrefetchScalarGridSpec(
        num_scalar_prefetch=1,
        grid=(b, nh),
        in_specs=[pl.BlockSpec((1, HEAD_DIM), small)] * 4 + [pl.BlockSpec((1, hw), small)]
        + [head_spec] * 4 + [tok_spec] * 4 + [hbm_spec] * 2,
        out_specs=[head_spec, pl.BlockSpec((n_tok, nh, hw), lambda bi, h, pt: (0, 0, 0))],
        scratch_shapes=[pltpu.VMEM((hw, s), BF16), pltpu.VMEM((s, 2 * hw), BF16)] + dec_scratch,
    )
    return pl.pallas_call(
        functools.partial(_attn_kernel, tq=tq, lam_init=lam_init, layer=layer, tok0=tok0, n_steps=n_steps),
        grid_spec=grid_spec,
        out_shape=[jax.ShapeDtypeStruct((b, s, aw), BF16), jax.ShapeDtypeStruct((n_tok, nh, hw), F32)],
        compiler_params=pltpu.CompilerParams(
            dimension_semantics=("arbitrary", "arbitrary"), vmem_limit_bytes=VMEM_LIMIT),
        name="attention",
    )(page_table, *lams, g_subln.reshape(1, hw), q, k, v, sa, *dec_ops, *caches)


def _conv_kernel(gc_ref, gh_ref, wdw_ref, bdw_ref, gcl_ref, bcl_ref, wpw_ref, sc_ref, o_ref,
                 sh_scr, c_scr, *, halo, kw, sub, rc, lc):
    i = pl.program_id(1)
    ts, cw = gc_ref.shape
    off = halo - (kw - 1)
    n_copy = halo + ts - SUBLANES
    sh_scr[0, 0:halo, :] = jnp.where(i == 0, 0.0, gh_ref[...])
    sh_scr[0, halo:halo + ts, :] = gc_ref[...]
    for ph in range(1, SUBLANES):
        sh_scr[ph, 0:n_copy, :] = sh_scr[0, ph:ph + n_copy, :]
    for s0 in range(0, ts, sub):
        for c0 in range(0, cw, lc):
            for r0 in range(s0, s0 + sub, rc):
                acc = jnp.broadcast_to(bdw_ref[:, c0:c0 + lc], (rc, lc))
                for j in range(kw):
                    ph, base = (off + j) % SUBLANES, (off + j) // SUBLANES * SUBLANES
                    acc = acc + wdw_ref[j:j + 1, c0:c0 + lc] * sh_scr[ph, r0 + base:r0 + base + rc, c0:c0 + lc]
                c_scr[r0:r0 + rc, c0:c0 + lc] = acc
        rows = slice(s0, s0 + sub)
        c = c_scr[rows, :]
        mu = jnp.mean(c, axis=-1, keepdims=True)
        var = jnp.mean(jnp.square(c - mu), axis=-1, keepdims=True)
        y = (c - mu) * lax.rsqrt(var + EPS) * gcl_ref[...] + bcl_ref[...]
        pw = jnp.dot(_silu(y).astype(BF16), wpw_ref[...], preferred_element_type=F32)
        o_ref[rows, :] = (pw * sc_ref[rows, :].astype(F32)).astype(o_ref.dtype)


def _prompt_conv(g, sc, w_dw, b_dw, g_cln, b_cln, w_pw_bf, ts, sub):
    b, s, cw = g.shape
    kw = w_dw.shape[0]
    halo = -(-(kw - 1) // SUBLANES) * SUBLANES
    assert ts % halo == 0 and s % ts == 0 and ts % sub == 0
    hb = ts // halo
    vec = lambda a: a.reshape(1, cw)
    full = lambda bi, i: (0, 0)
    tile = lambda bi, i: (bi, i, 0)
    return pl.pallas_call(
        functools.partial(_conv_kernel, halo=halo, kw=kw, sub=sub, rc=64, lc=2 * LANES),
        grid=(b, s // ts),
        in_specs=[
            pl.BlockSpec((None, ts, cw), tile),
            pl.BlockSpec((None, halo, cw), lambda bi, i: (bi, jnp.maximum(i * hb - 1, 0), 0)),
            pl.BlockSpec((kw, cw), full),
            pl.BlockSpec((1, cw), full),
            pl.BlockSpec((1, cw), full),
            pl.BlockSpec((1, cw), full),
            pl.BlockSpec((cw, cw), full),
            pl.BlockSpec((None, ts, cw), tile),
        ],
        out_specs=pl.BlockSpec((None, ts, cw), tile),
        out_shape=jax.ShapeDtypeStruct((b, s, cw), BF16),
        scratch_shapes=[pltpu.VMEM((SUBLANES, halo + ts, cw), F32), pltpu.VMEM((ts, cw), F32)],
        compiler_params=pltpu.CompilerParams(
            dimension_semantics=("parallel", "arbitrary"), vmem_limit_bytes=VMEM_LIMIT),
        name="prompt_conv",
    )(g, g, w_dw, vec(b_dw), vec(g_cln), vec(b_cln), w_pw_bf, sc)


def _decode_conv_kernel(hist_ref, wdw_ref, bdw_ref, gcl_ref, bcl_ref, wpw_ref, sc_ref, o_ref):
    kw = hist_ref.shape[0]
    acc = jnp.broadcast_to(bdw_ref[...], o_ref.shape)
    for j in range(kw):
        acc = acc + wdw_ref[j:j + 1, :] * hist_ref[j]
    mu = jnp.mean(acc, axis=-1, keepdims=True)
    var = jnp.mean(jnp.square(acc - mu), axis=-1, keepdims=True)
    y = (acc - mu) * lax.rsqrt(var + EPS) * gcl_ref[...] + bcl_ref[...]
    c = jnp.dot(_silu(y).astype(BF16), wpw_ref[...], preferred_element_type=F32)
    o_ref[...] = (c * sc_ref[...].astype(F32)).astype(o_ref.dtype)


def _decode_conv(hist_t, sc, w_dw, b_dw, g_cln, b_cln, w_pw_bf):
    kw, db, cw = hist_t.shape
    vec = lambda a: a.reshape(1, cw)
    return pl.pallas_call(
        _decode_conv_kernel,
        out_shape=jax.ShapeDtypeStruct((db, cw), BF16),
        compiler_params=pltpu.CompilerParams(vmem_limit_bytes=VMEM_LIMIT),
        name="decode_conv",
    )(hist_t, w_dw, vec(b_dw), vec(g_cln), vec(b_cln), w_pw_bf, sc)


def _out_rows(rows, x_ref, oa_ref, oc_ref, wout_ref):
    aw = oa_ref.shape[1]
    h = x_ref[rows, :]
    h = h + jnp.dot(oa_ref[rows, :], wout_ref[0:aw, :], preferred_element_type=F32)
    return h + jnp.dot(oc_ref[rows, :], wout_ref[aw:, :], preferred_element_type=F32)


def _ple_rows(rows, h, p_ref, gple_ref, wpg_ref, wple_ref, gfin_ref, y_ref, final_norm):
    r = _rms(h, gple_ref[...]).astype(BF16)
    gate = _sigmoid(jnp.dot(r, wpg_ref[...], preferred_element_type=F32))
    e = jnp.dot(p_ref[rows, :].astype(BF16), wple_ref[...], preferred_element_type=F32)
    h = h + gate * e
    y_ref[rows, :] = _rms(h, gfin_ref[...]) if final_norm else h


def _out_kernel(x_ref, oa_ref, oc_ref, p_ref, wout_ref, gple_ref, wpg_ref, wple_ref, gfin_ref, y_ref,
                *, final_norm):
    tm = oa_ref.shape[0]
    sub = min(tm, OUT_SUBTILE)
    for r0 in range(0, tm, sub):
        rows = slice(r0, r0 + sub)
        h = _out_rows(rows, x_ref, oa_ref, oc_ref, wout_ref)
        _ple_rows(rows, h, p_ref, gple_ref, wpg_ref, wple_ref, gfin_ref, y_ref, final_norm)


def _out_decode_kernel(pt_ref, lq1, lk1, lq2, lk2, gs_ref,
                       x_ref, oa_ref, oc_ref, p_ref, wout_ref, gple_ref, wpg_ref, wple_ref, gfin_ref,
                       qd_ref, knd_ref, vnd_ref, sad_ref, ck_hbm, cv_hbm, y_ref, od_ref,
                       kbuf, vbuf, sems, m_scr, l_scr, acca_scr, accb_scr,
                       *, final_norm, lam_init, layer, tok0, n_steps):
    rows = slice(0, oa_ref.shape[0])
    dec = _DecodeHost(pt_ref, _lam(lq1, lk1, lq2, lk2, lam_init), gs_ref[...],
                      qd_ref, knd_ref, vnd_ref, sad_ref, ck_hbm, cv_hbm, od_ref,
                      kbuf, vbuf, sems, m_scr, l_scr, acca_scr, accb_scr,
                      lam_init=lam_init, layer=layer, tok0=tok0, step=pl.program_id(0), n_steps=n_steps)
    dec.begin()
    dec.wait_part(0)
    scores = [dec.page_scores(0, i) for i in range(dec.part)]
    h = _out_rows(rows, x_ref, oa_ref, oc_ref, wout_ref)
    for i in range(dec.part):
        dec.page_update(0, i, scores[i])
    dec.refill_part(0)
    dec.wait_part(1)
    scores = [dec.page_scores(1, i) for i in range(dec.part)]
    _ple_rows(rows, h, p_ref, gple_ref, wpg_ref, wple_ref, gfin_ref, y_ref, final_norm)
    for i in range(dec.part):
        dec.page_update(1, i, scores[i])
    dec.refill_part(1)
    dec.finish()


def _outproj(x2d, oa, oc, p2d, w_out_bf, g_ple, w_pg_bf, w_ple_bf, g_final, tm, final_norm, decode=None):
    m, d = x2d.shape
    aw, cw, pd = oa.shape[1], oc.shape[1], p2d.shape[1]
    row = lambda i, *_: (i, 0)
    full = lambda i, *_: (0, 0)
    once = pl.Buffered(1)
    in_specs = [
        pl.BlockSpec((tm, d), row),
        pl.BlockSpec((tm, aw), row),
        pl.BlockSpec((tm, cw), row),
        pl.BlockSpec((tm, pd), row),
        pl.BlockSpec((aw + cw, d), full, pipeline_mode=once),
        pl.BlockSpec((1, d), full),
        pl.BlockSpec((d, d), full, pipeline_mode=once),
        pl.BlockSpec((pd, d), full, pipeline_mode=once),
        pl.BlockSpec((1, d), full),
    ]
    operands = (x2d, oa, oc, p2d, w_out_bf, g_ple.reshape(1, d), w_pg_bf, w_ple_bf, g_final.reshape(1, d))
    y_spec, y_shape = pl.BlockSpec((tm, d), row), jax.ShapeDtypeStruct((m, d), F32)
    if decode is None:
        return pl.pallas_call(
            functools.partial(_out_kernel, final_norm=final_norm),
            grid=(m // tm,), in_specs=in_specs, out_specs=y_spec, out_shape=y_shape,
            compiler_params=pltpu.CompilerParams(
                dimension_semantics=("parallel",), vmem_limit_bytes=VMEM_LIMIT),
            name="outproj",
        )(*operands)
    dec_ops, caches, page_table, layer, tok0, n_tok, lams, g_subln, lam_init = decode
    _, nh, hw = dec_ops[0].shape
    n_pages, page = page_table.shape[1], caches[0].shape[2]
    n_steps = m // tm
    assert (n_tok * n_pages) % n_steps == 0
    tok_spec, hbm_spec, dec_scratch = _decode_host_specs(n_tok, tok0, nh, hw, page, n_tok * n_pages // n_steps)
    grid_spec = pltpu.PrefetchScalarGridSpec(
        num_scalar_prefetch=1,
        grid=(n_steps,),
        in_specs=[pl.BlockSpec((1, HEAD_DIM), full)] * 4 + [pl.BlockSpec((1, hw), full)] + in_specs
        + [tok_spec] * 4 + [hbm_spec] * 2,
        out_specs=[y_spec, pl.BlockSpec((n_tok, nh, hw), lambda i, pt: (0, 0, 0))],
        scratch_shapes=dec_scratch,
    )
    return pl.pallas_call(
        functools.partial(_out_decode_kernel, final_norm=final_norm, lam_init=lam_init, layer=layer,
                          tok0=tok0, n_steps=n_steps),
        grid_spec=grid_spec,
        out_shape=[y_shape, jax.ShapeDtypeStruct((n_tok, nh, hw), F32)],
        compiler_params=pltpu.CompilerParams(
            dimension_semantics=("arbitrary",), vmem_limit_bytes=VMEM_LIMIT),
        name="outproj_decode",
    )(page_table, *lams, g_subln.reshape(1, hw), *operands, *dec_ops, *caches)


def kernel(x_prompt, x_sample, p_prompt, p_sample, cache_k, cache_v, state_conv, page_table, w_in, g_norm, lam_q1, lam_k1, lam_q2, lam_k2, g_subln, w_dw, b_dw, g_cln, b_cln, w_pw, w_out, g_ple, w_pg, w_ple, g_final):
    b, s, d = x_prompt.shape
    db, ds, _ = x_sample.shape
    depth = w_in.shape[0]
    nh, hw = cache_k.shape[3], cache_k.shape[4]
    aw = nh * hw
    cw = w_dw.shape[2]
    kw = w_dw.shape[1]
    past = page_table.shape[1] * cache_k.shape[2]
    assert ds == 1 and aw == cw and hw == 2 * HEAD_DIM

    tab_p = _rope_tables(jnp.arange(s, dtype=jnp.int32))
    tab_s = _rope_tables(jnp.full((db * ds,), past, dtype=jnp.int32))

    h_p = x_prompt.reshape(b * s, d)
    h_s = x_sample.reshape(db * ds, d)
    outs = {k: [] for k in ("kp", "vp", "cp", "ks", "vs", "cs")}
    for i in range(depth):
        lam_init = _lam_init(i)
        last = i == depth - 1
        lams = tuple(a[i].reshape(1, HEAD_DIM) for a in (lam_q1, lam_k1, lam_q2, lam_k2))
        w_in_bf, w_pw_bf = w_in[i].astype(BF16), w_pw[i].astype(BF16)
        w_out_bf, w_pg_bf, w_ple_bf = w_out[i].astype(BF16), w_pg[i].astype(BF16), w_ple[i].astype(BF16)

        q, k, v, qd, kd, vd = _inproj(h_p, h_s, g_norm[i], w_in_bf, tab_p, tab_s, tm=512, seg=aw,
                                      kinds=("q", "k", "v"))
        sa, g, sc, sad, gd, scd = _inproj(h_p, h_s, g_norm[i], w_in_bf, tab_p, tab_s, tm=1024, seg=aw,
                                          kinds=("gate_a", "glu_a", "glu_b", "gate_c"))
        as3 = lambda a: a.reshape(b, s, -1)
        dec_ops = tuple(a.reshape(db, nh, hw) for a in (qd, kd, vd, sad))
        caches = (cache_k, cache_v)
        n_att = 3 * db // 4

        oa, oad_a = _attention(as3(q), as3(k), as3(v), as3(sa), dec_ops, caches, page_table, i, 0, n_att,
                               lams, g_subln[i], tq=256, lam_init=lam_init)
        oc = _prompt_conv(as3(g), as3(sc), w_dw[i], b_dw[i], g_cln[i], b_cln[i], w_pw_bf, ts=512, sub=128)
        h_p, oad_o = _outproj(h_p, oa.reshape(b * s, aw), oc.reshape(b * s, cw), p_prompt[i].reshape(b * s, -1),
                              w_out_bf, g_ple[i], w_pg_bf, w_ple_bf, g_final, tm=256, final_norm=last,
                              decode=(dec_ops, caches, page_table, i, n_att, db - n_att, lams, g_subln[i], lam_init))
        oad = jnp.concatenate([oad_a, oad_o], axis=0)
        outs["kp"].append(k.reshape(b, s, nh, hw))
        outs["vp"].append(v.reshape(b, s, nh, hw))
        outs["cp"].append(as3(g)[:, s - (kw - 1):])

        hist = jnp.concatenate([state_conv[i], gd.reshape(db, ds, cw)], axis=1)
        ocd = _decode_conv(hist.transpose(1, 0, 2), scd, w_dw[i], b_dw[i], g_cln[i], b_cln[i], w_pw_bf)
        h_s = _outproj(h_s, oad.reshape(db, aw).astype(BF16), ocd, p_sample[i].reshape(db * ds, -1),
                       w_out_bf, g_ple[i], w_pg_bf, w_ple_bf, g_final, tm=db * ds, final_norm=last)
        outs["ks"].append(kd.reshape(db, ds, nh, hw))
        outs["vs"].append(vd.reshape(db, ds, nh, hw))
        outs["cs"].append(hist[:, 1:])

    st = lambda key: jnp.stack(outs[key])
    return (h_p.reshape(b, s, d), h_s.reshape(db, ds, d),
            st("kp"), st("vp"), st("cp"), st("ks"), st("vs"), st("cs"))
```

```python
import functools
import math

import jax
import jax.numpy as jnp
from jax import lax
from jax.experimental import pallas as pl
from jax.experimental.pallas import tpu as pltpu

F32 = jnp.float32
BF16 = jnp.bfloat16

HEAD_DIM = 64
ROT_DIM = HEAD_DIM // 4
HALF_ROT = ROT_DIM // 2
ROPE_THETA = 500000.0
EPS = 1e-6
NEG_INF = -1e30
Q_SCALE = HEAD_DIM ** -0.5 * math.log2(math.e)

LANES = 128
SUBLANES = 8
VMEM_LIMIT = 60 * 1024 * 1024
MXU_DIM = 256
INPROJ_CHUNK = MXU_DIM
OUT_SUBTILE = 256
N_HALVES = 2
DECODE_REDUCE_WAYS = 4


def _sigmoid(x):
    return 1.0 / (1.0 + jnp.exp(-x))


def _silu(x):
    return x * _sigmoid(x)


def _rms(x, g):
    return x * lax.rsqrt(jnp.mean(x * x, axis=-1, keepdims=True) + EPS) * g


def _lam_init(layer_idx):
    return 0.8 - 0.6 * math.exp(-0.3 * layer_idx)


def _lam(lq1, lk1, lq2, lk2, lam_init):
    a = jnp.sum(lq1[...] * lk1[...], keepdims=True)
    b = jnp.sum(lq2[...] * lk2[...], keepdims=True)
    return jnp.exp(a) - jnp.exp(b) + lam_init


def _rope_tables(pos):
    inv = ROPE_THETA ** (-jnp.arange(0, ROT_DIM, 2, dtype=F32) / ROT_DIM)
    ang = pos.astype(F32)[:, None] * inv[None, :]
    d = jnp.arange(2 * HEAD_DIM) % HEAD_DIM
    f = d % HALF_ROT
    cos_f = jnp.cos(ang)[:, f]
    sin_f = jnp.sin(ang)[:, f]
    lo = (d < HALF_ROT)[None, :]
    hi = ((d >= HALF_ROT) & (d < ROT_DIM))[None, :]
    cos_t = jnp.where(lo | hi, cos_f, 1.0)
    s_next = jnp.where(lo, -sin_f, 0.0)
    s_prev = jnp.where(hi, sin_f, 0.0)
    return cos_t, s_next, s_prev


SEGMENTS = ("q", "k", "v", "gate_a", "glu_a", "glu_b", "gate_c")
SEGMENT_DTYPE = {"q": BF16, "k": F32, "v": F32, "gate_a": BF16, "glu_b": F32, "gate_c": BF16}


def _inproj_kernel(*refs, kinds):
    (x_ref, xd_ref, gn_ref, w_ref, cos_ref, snext_ref, sprev_ref, cosd_ref, snextd_ref, sprevd_ref), refs = refs[:10], refs[10:]
    stored = [k for k in kinds if k in SEGMENT_DTYPE]
    out_p = dict(zip(stored, refs[:len(stored)]))
    out_d = dict(zip(stored, refs[len(stored):2 * len(stored)]))
    u_scr, *glu_scr = refs[2 * len(stored):]
    n = pl.program_id(1)
    tm, seg = x_ref.shape[0], w_ref.shape[1]

    def segment(epilogue):
        u = u_scr[...]
        for c0 in range(0, seg, INPROJ_CHUNK):
            cols = slice(c0, c0 + INPROJ_CHUNK)
            z = jnp.dot(u, w_ref[:, cols], preferred_element_type=F32)
            epilogue(z, cols)

    def rope(z, scale):
        tables = [jnp.concatenate([t[...], td[...]], axis=0)
                  for t, td in ((cos_ref, cosd_ref), (snext_ref, snextd_ref), (sprev_ref, sprevd_ref))]
        cos_t, s_next, s_prev = tables
        heads = []
        for h in range(z.shape[1] // LANES):
            zh = z[:, h * LANES:(h + 1) * LANES]
            nxt = pltpu.roll(zh, LANES - HALF_ROT, 1)
            prv = pltpu.roll(zh, HALF_ROT, 1)
            heads.append((zh * cos_t + nxt * s_next + prv * s_prev) * scale)
        return jnp.concatenate(heads, axis=1)

    def store(kind, fn):
        def epilogue(z, cols):
            r = fn(z, cols)
            out_p[kind][:, cols] = r[0:tm].astype(out_p[kind].dtype)
            out_d[kind][:, cols] = r[tm:].astype(out_d[kind].dtype)
        return epilogue

    epilogues = {
        "q": store("q", lambda z, cols: rope(z, Q_SCALE)),
        "k": store("k", lambda z, cols: rope(z, 1.0)),
        "v": store("v", lambda z, cols: z),
        "gate_a": store("gate_a", lambda z, cols: _silu(z)),
        "gate_c": store("gate_c", lambda z, cols: _silu(z)),
        "glu_b": store("glu_b", lambda z, cols: glu_scr[0][:, cols] * _sigmoid(z)),
    }

    def keep_glu_a(z, cols):
        glu_scr[0][:, cols] = z

    for idx, kind in enumerate(kinds):
        @pl.when(n == idx)
        def _(idx=idx, kind=kind):
            if idx == 0:
                u_scr[0:tm, :] = _rms(x_ref[...], gn_ref[...]).astype(BF16)
                u_scr[tm:, :] = _rms(xd_ref[...], gn_ref[...]).astype(BF16)
            segment(keep_glu_a if kind == "glu_a" else epilogues[kind])


def _inproj(x2d, xd2d, g_norm, w_in_bf, tab_p, tab_d, tm, seg, kinds):
    m, d = x2d.shape
    md = xd2d.shape[0]
    seg0 = SEGMENTS.index(kinds[0])
    assert SEGMENTS[seg0:seg0 + len(kinds)] == tuple(kinds) and m % tm == 0 and w_in_bf.shape[1] == seg * len(SEGMENTS)
    t_blocks = tab_p[0].shape[0] // tm
    row = lambda i, n: (i, 0)
    fixed = lambda i, n: (0, 0)
    tab = lambda i, n: (i % t_blocks, 0)
    stored = [k for k in kinds if k in SEGMENT_DTYPE]
    has_glu = "glu_a" in kinds
    assert has_glu == ("glu_b" in kinds)
    return pl.pallas_call(
        functools.partial(_inproj_kernel, kinds=tuple(kinds)),
        grid=(m // tm, len(kinds)),
        in_specs=[
            pl.BlockSpec((tm, d), row),
            pl.BlockSpec((md, d), fixed),
            pl.BlockSpec((1, d), fixed),
            pl.BlockSpec((d, seg), lambda i, n: (0, seg0 + n)),
        ] + [pl.BlockSpec((tm, LANES), tab)] * 3 + [pl.BlockSpec((md, LANES), fixed)] * 3,
        out_specs=[pl.BlockSpec((tm, seg), row)] * len(stored) + [pl.BlockSpec((md, seg), fixed)] * len(stored),
        out_shape=[jax.ShapeDtypeStruct((m, seg), SEGMENT_DTYPE[k]) for k in stored]
        + [jax.ShapeDtypeStruct((md, seg), SEGMENT_DTYPE[k]) for k in stored],
        scratch_shapes=[pltpu.VMEM((tm + md, d), BF16)] + ([pltpu.VMEM((tm + md, seg), F32)] if has_glu else []),
        compiler_params=pltpu.CompilerParams(
            dimension_semantics=("arbitrary", "arbitrary"), vmem_limit_bytes=VMEM_LIMIT),
        name="inproj_" + kinds[0],
    )(x2d, xd2d, g_norm.reshape(1, d), w_in_bf, *tab_p, *tab_d)


class _DecodeHost:
    def __init__(self, pt_ref, lam, gs, qd_ref, knd_ref, vnd_ref, sad_ref, ck_hbm, cv_hbm, od_ref,
                 kbuf, vbuf, sems, m_scr, l_scr, acca_scr, accb_scr,
                 *, lam_init, layer, tok0, step, n_steps):
        self.pt_ref, self.lam, self.gs, self.lam_init, self.layer = pt_ref, lam, gs, lam_init, layer
        self.knd_ref, self.vnd_ref, self.sad_ref, self.od_ref = knd_ref, vnd_ref, sad_ref, od_ref
        self.ck_hbm, self.cv_hbm, self.kbuf, self.vbuf, self.sems = ck_hbm, cv_hbm, kbuf, vbuf, sems
        self.state = (m_scr, l_scr, acca_scr, accb_scr)
        self.tok0, self.step, self.n_steps = tok0, step, n_steps
        self.n_tok, self.nh, self.hw = qd_ref.shape
        self.n_parts, self.part = kbuf.shape[0], kbuf.shape[1]
        self.per_tok = self.n_parts * self.part // self.n_tok
        assert self.per_tok * self.n_tok == self.n_parts * self.part
        assert self.per_tok * n_steps == pt_ref.shape[1]
        hw = self.hw
        d_map = lax.broadcasted_iota(jnp.int32, (hw, hw), 0) // HEAD_DIM
        n_map = lax.broadcasted_iota(jnp.int32, (hw, hw), 1) // HEAD_DIM
        self.sel = (d_map == n_map).astype(BF16)
        self.qd = qd_ref[...].astype(F32)

    def _page_copies(self, step_idx, hb, i):
        j = hb * self.part + i
        pid = self.pt_ref[self.tok0 + j // self.per_tok, step_idx * self.per_tok + j % self.per_tok]
        return (pltpu.make_async_copy(self.ck_hbm.at[self.layer, pid], self.kbuf.at[hb, i], self.sems.at[0, hb, i]),
                pltpu.make_async_copy(self.cv_hbm.at[self.layer, pid], self.vbuf.at[hb, i], self.sems.at[1, hb, i]))

    def _start_part(self, step_idx, hb):
        for i in range(self.part):
            for copy in self._page_copies(step_idx, hb, i):
                copy.start()

    def wait_part(self, hb):
        for i in range(self.part):
            for copy in self._page_copies(self.step, hb, i):
                copy.wait()

    def refill_part(self, hb):
        @pl.when(self.step + 1 < self.n_steps)
        def _():
            self._start_part(self.step + 1, hb)

    def begin(self):
        @pl.when(self.step == 0)
        def _():
            for hb in range(self.n_parts):
                self._start_part(0, hb)
            m_scr, l_scr, acca_scr, accb_scr = self.state
            m_scr[...] = jnp.full(m_scr.shape, NEG_INF, F32)
            l_scr[...] = jnp.zeros(l_scr.shape, F32)
            acca_scr[...] = jnp.zeros(acca_scr.shape, F32)
            accb_scr[...] = jnp.zeros(accb_scr.shape, F32)

    @staticmethod
    def _swap_maps(x):
        return pltpu.roll(x, HEAD_DIM, x.ndim - 1)

    @staticmethod
    def _reduce(fn, x):
        ways = DECODE_REDUCE_WAYS if x.shape[0] % DECODE_REDUCE_WAYS == 0 else 1
        return fn(fn(x.reshape(ways, x.shape[0] // ways, *x.shape[1:]), axis=1), axis=0)

    def _scores(self, tok, kp):
        rows = kp.shape[0]
        prod = (kp * self.qd[tok][None]).reshape(rows * self.nh, self.hw)
        return jnp.dot(prod.astype(BF16), self.sel, preferred_element_type=F32)

    def _update(self, t, s, vp):
        m_scr, l_scr, acca_scr, accb_scr = self.state
        nh, hw = self.nh, self.hw
        rows = vp.shape[0]
        m_old = m_scr[t]
        m_new = jnp.maximum(m_old, self._reduce(jnp.max, s.reshape(rows, nh, hw)))
        alpha = jnp.exp2(m_old - m_new)
        p2 = jnp.exp2(s.reshape(rows, nh, hw) - m_new[None]).reshape(rows * nh, hw)
        p = p2.reshape(rows, nh, hw)
        p_sw = self._swap_maps(p2).reshape(rows, nh, hw)
        l_scr[t] = alpha * l_scr[t] + self._reduce(jnp.sum, p)
        acca_scr[t] = alpha * acca_scr[t] + self._reduce(jnp.sum, p * vp)
        accb_scr[t] = self._swap_maps(alpha) * accb_scr[t] + self._reduce(jnp.sum, p_sw * vp)
        m_scr[t] = m_new

    def page_scores(self, hb, i):
        return self._scores((hb * self.part + i) // self.per_tok, self.kbuf[hb, i])

    def page_update(self, hb, i, s):
        self._update(hb * self.part + i, s, self.vbuf[hb, i])

    def finish(self):
        @pl.when(self.step == self.n_steps - 1)
        def _():
            m_scr, l_scr, acca_scr, accb_scr = self.state
            nh, hw, per_tok = self.nh, self.hw, self.per_tok
            lo = lax.broadcasted_iota(jnp.int32, (nh, hw), 1) < HEAD_DIM
            for tok in range(self.n_tok):
                t0 = tok * per_tok
                self._update(t0, self._scores(tok, self.knd_ref[tok][None]), self.vnd_ref[tok][None])
                m_all = m_scr[t0:t0 + per_tok]
                m = jnp.max(m_all, axis=0)
                w = jnp.exp2(m_all - m[None])
                w_sw = self._swap_maps(w.reshape(per_tok * nh, hw)).reshape(per_tok, nh, hw)
                l = jnp.sum(w * l_scr[t0:t0 + per_tok], axis=0)
                acc_a = jnp.sum(w * acca_scr[t0:t0 + per_tok], axis=0)
                acc_b = jnp.sum(w_sw * accb_scr[t0:t0 + per_tok], axis=0)
                na = acc_a / l
                nb = acc_b / self._swap_maps(l)
                o = jnp.where(lo, na, nb) - self.lam * jnp.where(lo, nb, na)
                o = _rms(o, self.gs) * (1.0 - self.lam_init)
                self.od_ref[tok] = o * self.sad_ref[tok].astype(F32)


def _decode_host_specs(n_tok, tok0, nh, hw, page, pages_per_step):
    assert tok0 % n_tok == 0 and pages_per_step % N_HALVES == 0
    tok_spec = pl.BlockSpec((n_tok, nh, hw), lambda *_: (tok0 // n_tok, 0, 0))
    part = pages_per_step // N_HALVES
    scratch = [
        pltpu.VMEM((N_HALVES, part, page, nh, hw), F32), pltpu.VMEM((N_HALVES, part, page, nh, hw), F32),
        pltpu.SemaphoreType.DMA((2, N_HALVES, part)),
    ] + [pltpu.VMEM((pages_per_step, nh, hw), F32)] * 4
    return tok_spec, pl.BlockSpec(memory_space=pl.ANY), scratch


def _attn_kernel(pt_ref, lq1, lk1, lq2, lk2, gs_ref, q_ref, k_ref, v_ref, sa_ref,
                 qd_ref, knd_ref, vnd_ref, sad_ref, ck_hbm, cv_hbm, o_ref, od_ref,
                 kt_scr, vb_scr, kbuf, vbuf, sems, m_scr, l_scr, acca_scr, accb_scr,
                 *, tq, lam_init, layer, tok0, n_steps):
    s_len, hw = k_ref.shape
    step = pl.program_id(0) * pl.num_programs(1) + pl.program_id(1)
    lam = _lam(lq1, lk1, lq2, lk2, lam_init)
    gs = gs_ref[...]
    dec = _DecodeHost(pt_ref, lam, gs, qd_ref, knd_ref, vnd_ref, sad_ref, ck_hbm, cv_hbm, od_ref,
                      kbuf, vbuf, sems, m_scr, l_scr, acca_scr, accb_scr,
                      lam_init=lam_init, layer=layer, tok0=tok0, step=step, n_steps=n_steps)
    dec.begin()

    kt_scr[...] = k_ref[...].T.astype(BF16)
    vb_scr[:, 0:hw] = v_ref[...].astype(BF16)
    vb_scr[:, hw:2 * hw] = jnp.ones((s_len, hw), BF16)
    lane = lax.broadcasted_iota(jnp.int32, (tq, hw), 1)
    q_row = lax.broadcasted_iota(jnp.int32, (2 * tq, tq), 0) % tq
    k_col = lax.broadcasted_iota(jnp.int32, (2 * tq, tq), 1)
    causal = k_col <= q_row

    def attention_tile(i):
        r0, c_diag = i * tq, i * tq
        q = q_ref[r0:r0 + tq, :]
        zero = jnp.zeros_like(q)
        q2 = jnp.concatenate(
            [jnp.where(lane < HEAD_DIM, q, zero), jnp.where(lane >= HEAD_DIM, q, zero)], axis=0)
        s = jnp.dot(q2, kt_scr[:, 0:c_diag + tq], preferred_element_type=F32)
        s_diag = jnp.where(causal, s[:, c_diag:], NEG_INF)
        m = jnp.max(s_diag, axis=-1, keepdims=True)
        if i > 0:
            m = jnp.maximum(m, jnp.max(s[:, 0:c_diag], axis=-1, keepdims=True))
        p = jnp.exp2(s_diag - m).astype(BF16)
        if i > 0:
            p = jnp.concatenate([jnp.exp2(s[:, 0:c_diag] - m).astype(BF16), p], axis=1)
        acc = jnp.dot(p, vb_scr[0:c_diag + tq, :], preferred_element_type=F32)
        o = acc[0:tq, 0:hw] / acc[0:tq, hw:] - lam * (acc[tq:, 0:hw] / acc[tq:, hw:])
        o = _rms(o, gs) * (1.0 - lam_init)
        o_ref[r0:r0 + tq, :] = (o * sa_ref[r0:r0 + tq, :].astype(F32)).astype(o_ref.dtype)

    def part_of_step(q_tiles, hb):
        work = sum(i + 1 for i in q_tiles)
        work_done = pages_done = 0
        for i in q_tiles:
            attention_tile(i)
            work_done += i + 1
            while pages_done < work_done * dec.part // work:
                dec.page_update(hb, pages_done, dec.page_scores(hb, pages_done))
                pages_done += 1

    n_q = s_len // tq
    tiles_upto = lambda n: n * (n + 1) // 2
    split = min(range(n_q + 1), key=lambda n: abs(2 * tiles_upto(n) - tiles_upto(n_q)))
    for hb, q_tiles in enumerate((range(0, split), range(split, n_q))):
        dec.wait_part(hb)
        part_of_step(q_tiles, hb)
        dec.refill_part(hb)
    dec.finish()


def _attention(q, k, v, sa, dec_ops, caches, page_table, layer, tok0, n_tok, lams, g_subln, tq, lam_init):
    b, s, aw = q.shape
    _, nh, hw = dec_ops[0].shape
    n_pages, page = page_table.shape[1], caches[0].shape[2]
    n_steps = b * nh
    assert aw == nh * hw and (n_tok * n_pages) % n_steps == 0
    pages_per_step = n_tok * n_pages // n_steps
    head = lambda bi, h, pt: (bi, 0, h)
    small = lambda bi, h, pt: (0, 0)
    head_spec = pl.BlockSpec((None, s, hw), head)
    tok_spec, hbm_spec, dec_scratch = _decode_host_specs(n_tok, tok0, nh, hw, page, pages_per_step)
    grid_spec = pltpu.PrefetchScalarGridSpec(
        num_scalar_prefetch=1,
        grid=(b, nh),
        in_specs=[pl.BlockSpec((1, HEAD_DIM), small)] * 4 + [pl.BlockSpec((1, hw), small)]
        + [head_spec] * 4 + [tok_spec] * 4 + [hbm_spec] * 2,
        out_specs=[head_spec, pl.BlockSpec((n_tok, nh, hw), lambda bi, h, pt: (0, 0, 0))],
        scratch_shapes=[pltpu.VMEM((hw, s), BF16), pltpu.VMEM((s, 2 * hw), BF16)] + dec_scratch,
    )
    return pl.pallas_call(
        functools.partial(_attn_kernel, tq=tq, lam_init=lam_init, layer=layer, tok0=tok0, n_steps=n_steps),
        grid_spec=grid_spec,
        out_shape=[jax.ShapeDtypeStruct((b, s, aw), BF16), jax.ShapeDtypeStruct((n_tok, nh, hw), F32)],
        compiler_params=pltpu.CompilerParams(
            dimension_semantics=("arbitrary", "arbitrary"), vmem_limit_bytes=VMEM_LIMIT),
        name="attention",
    )(page_table, *lams, g_subln.reshape(1, hw), q, k, v, sa, *dec_ops, *caches)


def _conv_kernel(gc_ref, gh_ref, wdw_ref, bdw_ref, gcl_ref, bcl_ref, wpw_ref, sc_ref, o_ref,
                 sh_scr, c_scr, *, halo, kw, sub, rc, lc):
    i = pl.program_id(1)
    ts, cw = gc_ref.shape
    off = halo - (kw - 1)
    n_copy = halo + ts - SUBLANES
    sh_scr[0, 0:halo, :] = jnp.where(i == 0, 0.0, gh_ref[...])
    sh_scr[0, halo:halo + ts, :] = gc_ref[...]
    for ph in range(1, SUBLANES):
        sh_scr[ph, 0:n_copy, :] = sh_scr[0, ph:ph + n_copy, :]
    for s0 in range(0, ts, sub):
        for c0 in range(0, cw, lc):
            for r0 in range(s0, s0 + sub, rc):
                acc = jnp.broadcast_to(bdw_ref[:, c0:c0 + lc], (rc, lc))
                for j in range(kw):
                    ph, base = (off + j) % SUBLANES, (off + j) // SUBLANES * SUBLANES
                    acc = acc + wdw_ref[j:j + 1, c0:c0 + lc] * sh_scr[ph, r0 + base:r0 + base + rc, c0:c0 + lc]
                c_scr[r0:r0 + rc, c0:c0 + lc] = acc
        rows = slice(s0, s0 + sub)
        c = c_scr[rows, :]
        mu = jnp.mean(c, axis=-1, keepdims=True)
        var = jnp.mean(jnp.square(c - mu), axis=-1, keepdims=True)
        y = (c - mu) * lax.rsqrt(var + EPS) * gcl_ref[...] + bcl_ref[...]
        pw = jnp.dot(_silu(y).astype(BF16), wpw_ref[...], preferred_element_type=F32)
        o_ref[rows, :] = (pw * sc_ref[rows, :].astype(F32)).astype(o_ref.dtype)


def _prompt_conv(g, sc, w_dw, b_dw, g_cln, b_cln, w_pw_bf, ts, sub):
    b, s, cw = g.shape
    kw = w_dw.shape[0]
    halo = -(-(kw - 1) // SUBLANES) * SUBLANES
    assert ts % halo == 0 and s % ts == 0 and ts % sub == 0
    hb = ts // halo
    vec = lambda a: a.reshape(1, cw)
    full = lambda bi, i: (0, 0)
    tile = lambda bi, i: (bi, i, 0)
    return pl.pallas_call(
        functools.partial(_conv_kernel, halo=halo, kw=kw, sub=sub, rc=64, lc=2 * LANES),
        grid=(b, s // ts),
        in_specs=[
            pl.BlockSpec((None, ts, cw), tile),
            pl.BlockSpec((None, halo, cw), lambda bi, i: (bi, jnp.maximum(i * hb - 1, 0), 0)),
            pl.BlockSpec((kw, cw), full),
            pl.BlockSpec((1, cw), full),
            pl.BlockSpec((1, cw), full),
            pl.BlockSpec((1, cw), full),
            pl.BlockSpec((cw, cw), full),
            pl.BlockSpec((None, ts, cw), tile),
        ],
        out_specs=pl.BlockSpec((None, ts, cw), tile),
        out_shape=jax.ShapeDtypeStruct((b, s, cw), BF16),
        scratch_shapes=[pltpu.VMEM((SUBLANES, halo + ts, cw), F32), pltpu.VMEM((ts, cw), F32)],
        compiler_params=pltpu.CompilerParams(
            dimension_semantics=("parallel", "arbitrary"), vmem_limit_bytes=VMEM_LIMIT),
        name="prompt_conv",
    )(g, g, w_dw, vec(b_dw), vec(g_cln), vec(b_cln), w_pw_bf, sc)


def _decode_conv_kernel(hist_ref, wdw_ref, bdw_ref, gcl_ref, bcl_ref, wpw_ref, sc_ref, o_ref):
    kw = hist_ref.shape[0]
    acc = jnp.broadcast_to(bdw_ref[...], o_ref.shape)
    for j in range(kw):
        acc = acc + wdw_ref[j:j + 1, :] * hist_ref[j]
    mu = jnp.mean(acc, axis=-1, keepdims=True)
    var = jnp.mean(jnp.square(acc - mu), axis=-1, keepdims=True)
    y = (acc - mu) * lax.rsqrt(var + EPS) * gcl_ref[...] + bcl_ref[...]
    c = jnp.dot(_silu(y).astype(BF16), wpw_ref[...], preferred_element_type=F32)
    o_ref[...] = (c * sc_ref[...].astype(F32)).astype(o_ref.dtype)


def _decode_conv(hist_t, sc, w_dw, b_dw, g_cln, b_cln, w_pw_bf):
    kw, db, cw = hist_t.shape
    vec = lambda a: a.reshape(1, cw)
    return pl.pallas_call(
        _decode_conv_kernel,
        out_shape=jax.ShapeDtypeStruct((db, cw), BF16),
        compiler_params=pltpu.CompilerParams(vmem_limit_bytes=VMEM_LIMIT),
        name="decode_conv",
    )(hist_t, w_dw, vec(b_dw), vec(g_cln), vec(b_cln), w_pw_bf, sc)


def _out_rows(rows, x_ref, oa_ref, oc_ref, wout_ref):
    aw = oa_ref.shape[1]
    h = x_ref[rows, :]
    h = h + jnp.dot(oa_ref[rows, :], wout_ref[0:aw, :], preferred_element_type=F32)
    return h + jnp.dot(oc_ref[rows, :], wout_ref[aw:, :], preferred_element_type=F32)


def _ple_rows(rows, h, p_ref, gple_ref, wpg_ref, wple_ref, gfin_ref, y_ref, final_norm):
    r = _rms(h, gple_ref[...]).astype(BF16)
    gate = _sigmoid(jnp.dot(r, wpg_ref[...], preferred_element_type=F32))
    e = jnp.dot(p_ref[rows, :].astype(BF16), wple_ref[...], preferred_element_type=F32)
    h = h + gate * e
    y_ref[rows, :] = _rms(h, gfin_ref[...]) if final_norm else h


def _out_kernel(x_ref, oa_ref, oc_ref, p_ref, wout_ref, gple_ref, wpg_ref, wple_ref, gfin_ref, y_ref,
                *, final_norm):
    tm = oa_ref.shape[0]
    sub = min(tm, OUT_SUBTILE)
    for r0 in range(0, tm, sub):
        rows = slice(r0, r0 + sub)
        h = _out_rows(rows, x_ref, oa_ref, oc_ref, wout_ref)
        _ple_rows(rows, h, p_ref, gple_ref, wpg_ref, wple_ref, gfin_ref, y_ref, final_norm)


def _out_decode_kernel(pt_ref, lq1, lk1, lq2, lk2, gs_ref,
                       x_ref, oa_ref, oc_ref, p_ref, wout_ref, gple_ref, wpg_ref, wple_ref, gfin_ref,
                       qd_ref, knd_ref, vnd_ref, sad_ref, ck_hbm, cv_hbm, y_ref, od_ref,
                       kbuf, vbuf, sems, m_scr, l_scr, acca_scr, accb_scr,
                       *, final_norm, lam_init, layer, tok0, n_steps):
    rows = slice(0, oa_ref.shape[0])
    dec = _DecodeHost(pt_ref, _lam(lq1, lk1, lq2, lk2, lam_init), gs_ref[...],
                      qd_ref, knd_ref, vnd_ref, sad_ref, ck_hbm, cv_hbm, od_ref,
                      kbuf, vbuf, sems, m_scr, l_scr, acca_scr, accb_scr,
                      lam_init=lam_init, layer=layer, tok0=tok0, step=pl.program_id(0), n_steps=n_steps)
    dec.begin()
    dec.wait_part(0)
    scores = [dec.page_scores(0, i) for i in range(dec.part)]
    h = _out_rows(rows, x_ref, oa_ref, oc_ref, wout_ref)
    for i in range(dec.part):
        dec.page_update(0, i, scores[i])
    dec.refill_part(0)
    dec.wait_part(1)
    scores = [dec.page_scores(1, i) for i in range(dec.part)]
    _ple_rows(rows, h, p_ref, gple_ref, wpg_ref, wple_ref, gfin_ref, y_ref, final_norm)
    for i in range(dec.part):
        dec.page_update(1, i, scores[i])
    dec.refill_part(1)
    dec.finish()


def _outproj(x2d, oa, oc, p2d, w_out_bf, g_ple, w_pg_bf, w_ple_bf, g_final, tm, final_norm, decode=None):
    m, d = x2d.shape
    aw, cw, pd = oa.shape[1], oc.shape[1], p2d.shape[1]
    row = lambda i, *_: (i, 0)
    full = lambda i, *_: (0, 0)
    once = pl.Buffered(1)
    in_specs = [
        pl.BlockSpec((tm, d), row),
        pl.BlockSpec((tm, aw), row),
        pl.BlockSpec((tm, cw), row),
        pl.BlockSpec((tm, pd), row),
        pl.BlockSpec((aw + cw, d), full, pipeline_mode=once),
        pl.BlockSpec((1, d), full),
        pl.BlockSpec((d, d), full, pipeline_mode=once),
        pl.BlockSpec((pd, d), full, pipeline_mode=once),
        pl.BlockSpec((1, d), full),
    ]
    operands = (x2d, oa, oc, p2d, w_out_bf, g_ple.reshape(1, d), w_pg_bf, w_ple_bf, g_final.reshape(1, d))
    y_spec, y_shape = pl.BlockSpec((tm, d), row), jax.ShapeDtypeStruct((m, d), F32)
    if decode is None:
        return pl.pallas_call(
            functools.partial(_out_kernel, final_norm=final_norm),
            grid=(m // tm,), in_specs=in_specs, out_specs=y_spec, out_shape=y_shape,
            compiler_params=pltpu.CompilerParams(
                dimension_semantics=("parallel",), vmem_limit_bytes=VMEM_LIMIT),
            name="outproj",
        )(*operands)
    dec_ops, caches, page_table, layer, tok0, n_tok, lams, g_subln, lam_init = decode
    _, nh, hw = dec_ops[0].shape
    n_pages, page = page_table.shape[1], caches[0].shape[2]
    n_steps = m // tm
    assert (n_tok * n_pages) % n_steps == 0
    tok_spec, hbm_spec, dec_scratch = _decode_host_specs(n_tok, tok0, nh, hw, page, n_tok * n_pages // n_steps)
    grid_spec = pltpu.PrefetchScalarGridSpec(
        num_scalar_prefetch=1,
        grid=(n_steps,),
        in_specs=[pl.BlockSpec((1, HEAD_DIM), full)] * 4 + [pl.BlockSpec((1, hw), full)] + in_specs
        + [tok_spec] * 4 + [hbm_spec] * 2,
        out_specs=[y_spec, pl.BlockSpec((n_tok, nh, hw), lambda i, pt: (0, 0, 0))],
        scratch_shapes=dec_scratch,
    )
    return pl.pallas_call(
        functools.partial(_out_decode_kernel, final_norm=final_norm, lam_init=lam_init, layer=layer,
                          tok0=tok0, n_steps=n_steps),
        grid_spec=grid_spec,
        out_shape=[y_shape, jax.ShapeDtypeStruct((n_tok, nh, hw), F32)],
        compiler_params=pltpu.CompilerParams(
            dimension_semantics=("arbitrary",), vmem_limit_bytes=VMEM_LIMIT),
        name="outproj_decode",
    )(page_table, *lams, g_subln.reshape(1, hw), *operands, *dec_ops, *caches)


def kernel(x_prompt, x_sample, p_prompt, p_sample, cache_k, cache_v, state_conv, page_table, w_in, g_norm, lam_q1, lam_k1, lam_q2, lam_k2, g_subln, w_dw, b_dw, g_cln, b_cln, w_pw, w_out, g_ple, w_pg, w_ple, g_final):
    b, s, d = x_prompt.shape
    db, ds, _ = x_sample.shape
    depth = w_in.shape[0]
    nh, hw = cache_k.shape[3], cache_k.shape[4]
    aw = nh * hw
    cw = w_dw.shape[2]
    kw = w_dw.shape[1]
    past = page_table.shape[1] * cache_k.shape[2]
    assert ds == 1 and aw == cw and hw == 2 * HEAD_DIM

    tab_p = _rope_tables(jnp.arange(s, dtype=jnp.int32))
    tab_s = _rope_tables(jnp.full((db * ds,), past, dtype=jnp.int32))

    h_p = x_prompt.reshape(b * s, d)
    h_s = x_sample.reshape(db * ds, d)
    outs = {k: [] for k in ("kp", "vp", "cp", "ks", "vs", "cs")}
    for i in range(depth):
        lam_init = _lam_init(i)
        last = i == depth - 1
        lams = tuple(a[i].reshape(1, HEAD_DIM) for a in (lam_q1, lam_k1, lam_q2, lam_k2))
        w_in_bf, w_pw_bf = w_in[i].astype(BF16), w_pw[i].astype(BF16)
        w_out_bf, w_pg_bf, w_ple_bf = w_out[i].astype(BF16), w_pg[i].astype(BF16), w_ple[i].astype(BF16)

        q, k, v, qd, kd, vd = _inproj(h_p, h_s, g_norm[i], w_in_bf, tab_p, tab_s, tm=1024, seg=aw,
                                      kinds=("q", "k", "v"))
        sa, g, sc, sad, gd, scd = _inproj(h_p, h_s, g_norm[i], w_in_bf, tab_p, tab_s, tm=1024, seg=aw,
                                          kinds=("gate_a", "glu_a", "glu_b", "gate_c"))
        as3 = lambda a: a.reshape(b, s, -1)
        dec_ops = tuple(a.reshape(db, nh, hw) for a in (qd, kd, vd, sad))
        caches = (cache_k, cache_v)
        n_att = 3 * db // 4

        oa, oad_a = _attention(as3(q), as3(k), as3(v), as3(sa), dec_ops, caches, page_table, i, 0, n_att,
                               lams, g_subln[i], tq=256, lam_init=lam_init)
        oc = _prompt_conv(as3(g), as3(sc), w_dw[i], b_dw[i], g_cln[i], b_cln[i], w_pw_bf, ts=512, sub=128)
        h_p, oad_o = _outproj(h_p, oa.reshape(b * s, aw), oc.reshape(b * s, cw), p_prompt[i].reshape(b * s, -1),
                              w_out_bf, g_ple[i], w_pg_bf, w_ple_bf, g_final, tm=256, final_norm=last,
                              decode=(dec_ops, caches, page_table, i, n_att, db - n_att, lams, g_subln[i], lam_init))
        oad = jnp.concatenate([oad_a, oad_o], axis=0)
        outs["kp"].append(k.reshape(b, s, nh, hw))
        outs["vp"].append(v.reshape(b, s, nh, hw))
        outs["cp"].append(as3(g)[:, s - (kw - 1):])

        hist = jnp.concatenate([state_conv[i], gd.reshape(db, ds, cw)], axis=1)
        ocd = _decode_conv(hist.transpose(1, 0, 2), scd, w_dw[i], b_dw[i], g_cln[i], b_cln[i], w_pw_bf)
        h_s = _outproj(h_s, oad.reshape(db, aw).astype(BF16), ocd, p_sample[i].reshape(db * ds, -1),
                       w_out_bf, g_ple[i], w_pg_bf, w_ple_bf, g_final, tm=db * ds, final_norm=last)
        outs["ks"].append(kd.reshape(db, ds, nh, hw))
        outs["vs"].append(vd.reshape(db, ds, nh, hw))
        outs["cs"].append(hist[:, 1:])

    st = lambda key: jnp.stack(outs[key])
    return (h_p.reshape(b, s, d), h_s.reshape(db, ds, d),
            st("kp"), st("vp"), st("cp"), st("ks"), st("vs"), st("cs"))
```

```python
import functools
import math

import jax
import jax.numpy as jnp
from jax import lax
from jax.experimental import pallas as pl
from jax.experimental.pallas import tpu as pltpu

F32 = jnp.float32
BF16 = jnp.bfloat16

HEAD_DIM = 64
ROT_DIM = HEAD_DIM // 4
HALF_ROT = ROT_DIM // 2
ROPE_THETA = 500000.0
EPS = 1e-6
NEG_INF = -1e30
Q_SCALE = HEAD_DIM ** -0.5 * math.log2(math.e)

LANES = 128
SUBLANES = 8
VMEM_LIMIT = 60 * 1024 * 1024
MXU_DIM = 256
INPROJ_CHUNK = MXU_DIM
OUT_SUBTILE = 256
CONV_SUBTILE = 128
CONV_ROW_CHUNK = 64
CONV_LANE_CHUNK = 2 * LANES
N_HALVES = 2
DECODE_REDUCE_WAYS = 4


def _sigmoid(x):
    return 1.0 / (1.0 + jnp.exp(-x))


def _silu(x):
    return x * _sigmoid(x)


def _rms(x, g):
    return x * lax.rsqrt(jnp.mean(x * x, axis=-1, keepdims=True) + EPS) * g


def _lam_init(layer_idx):
    return 0.8 - 0.6 * math.exp(-0.3 * layer_idx)


def _lam(lq1, lk1, lq2, lk2, lam_init):
    a = jnp.sum(lq1[...] * lk1[...], keepdims=True)
    b = jnp.sum(lq2[...] * lk2[...], keepdims=True)
    return jnp.exp(a) - jnp.exp(b) + lam_init


def _rope_tables(pos):
    inv = ROPE_THETA ** (-jnp.arange(0, ROT_DIM, 2, dtype=F32) / ROT_DIM)
    ang = pos.astype(F32)[:, None] * inv[None, :]
    d = jnp.arange(2 * HEAD_DIM) % HEAD_DIM
    f = d % HALF_ROT
    cos_f = jnp.cos(ang)[:, f]
    sin_f = jnp.sin(ang)[:, f]
    lo = (d < HALF_ROT)[None, :]
    hi = ((d >= HALF_ROT) & (d < ROT_DIM))[None, :]
    cos_t = jnp.where(lo | hi, cos_f, 1.0)
    s_next = jnp.where(lo, -sin_f, 0.0)
    s_prev = jnp.where(hi, sin_f, 0.0)
    return cos_t, s_next, s_prev


SEGMENTS = ("q", "k", "v", "gate_a", "glu_a", "glu_b", "gate_c")
SEGMENT_DTYPE = {"q": BF16, "k": F32, "v": F32, "gate_a": BF16, "glu_b": F32, "gate_c": BF16}


def _inproj_kernel(*refs, kinds):
    (x_ref, xd_ref, gn_ref, w_ref, cos_ref, snext_ref, sprev_ref, cosd_ref, snextd_ref, sprevd_ref), refs = refs[:10], refs[10:]
    stored = [k for k in kinds if k in SEGMENT_DTYPE]
    out_p = dict(zip(stored, refs[:len(stored)]))
    out_d = dict(zip(stored, refs[len(stored):2 * len(stored)]))
    u_scr, *glu_scr = refs[2 * len(stored):]
    n = pl.program_id(1)
    tm, seg = x_ref.shape[0], w_ref.shape[1]

    def segment(epilogue):
        u = u_scr[...]
        for c0 in range(0, seg, INPROJ_CHUNK):
            cols = slice(c0, c0 + INPROJ_CHUNK)
            z = jnp.dot(u, w_ref[:, cols], preferred_element_type=F32)
            epilogue(z, cols)

    def rope(z, scale):
        tables = [jnp.concatenate([t[...], td[...]], axis=0)
                  for t, td in ((cos_ref, cosd_ref), (snext_ref, snextd_ref), (sprev_ref, sprevd_ref))]
        cos_t, s_next, s_prev = tables
        heads = []
        for h in range(z.shape[1] // LANES):
            zh = z[:, h * LANES:(h + 1) * LANES]
            nxt = pltpu.roll(zh, LANES - HALF_ROT, 1)
            prv = pltpu.roll(zh, HALF_ROT, 1)
            heads.append((zh * cos_t + nxt * s_next + prv * s_prev) * scale)
        return jnp.concatenate(heads, axis=1)

    def store(kind, fn):
        def epilogue(z, cols):
            r = fn(z, cols)
            out_p[kind][:, cols] = r[0:tm].astype(out_p[kind].dtype)
            out_d[kind][:, cols] = r[tm:].astype(out_d[kind].dtype)
        return epilogue

    epilogues = {
        "q": store("q", lambda z, cols: rope(z, Q_SCALE)),
        "k": store("k", lambda z, cols: rope(z, 1.0)),
        "v": store("v", lambda z, cols: z),
        "gate_a": store("gate_a", lambda z, cols: _silu(z)),
        "gate_c": store("gate_c", lambda z, cols: _silu(z)),
        "glu_b": store("glu_b", lambda z, cols: glu_scr[0][:, cols] * _sigmoid(z)),
    }

    def keep_glu_a(z, cols):
        glu_scr[0][:, cols] = z

    for idx, kind in enumerate(kinds):
        @pl.when(n == idx)
        def _(idx=idx, kind=kind):
            if idx == 0:
                u_scr[0:tm, :] = _rms(x_ref[...], gn_ref[...]).astype(BF16)
                u_scr[tm:, :] = _rms(xd_ref[...], gn_ref[...]).astype(BF16)
            segment(keep_glu_a if kind == "glu_a" else epilogues[kind])


def _inproj(x2d, xd2d, g_norm, w_in_bf, tab_p, tab_d, tm, seg, kinds):
    m, d = x2d.shape
    md = xd2d.shape[0]
    seg0 = SEGMENTS.index(kinds[0])
    assert SEGMENTS[seg0:seg0 + len(kinds)] == tuple(kinds) and m % tm == 0 and w_in_bf.shape[1] == seg * len(SEGMENTS)
    t_blocks = tab_p[0].shape[0] // tm
    row = lambda i, n: (i, 0)
    fixed = lambda i, n: (0, 0)
    tab = lambda i, n: (i % t_blocks, 0)
    stored = [k for k in kinds if k in SEGMENT_DTYPE]
    has_glu = "glu_a" in kinds
    assert has_glu == ("glu_b" in kinds)
    return pl.pallas_call(
        functools.partial(_inproj_kernel, kinds=tuple(kinds)),
        grid=(m // tm, len(kinds)),
        in_specs=[
            pl.BlockSpec((tm, d), row),
            pl.BlockSpec((md, d), fixed),
            pl.BlockSpec((1, d), fixed),
            pl.BlockSpec((d, seg), lambda i, n: (0, seg0 + n)),
        ] + [pl.BlockSpec((tm, LANES), tab)] * 3 + [pl.BlockSpec((md, LANES), fixed)] * 3,
        out_specs=[pl.BlockSpec((tm, seg), row)] * len(stored) + [pl.BlockSpec((md, seg), fixed)] * len(stored),
        out_shape=[jax.ShapeDtypeStruct((m, seg), SEGMENT_DTYPE[k]) for k in stored]
        + [jax.ShapeDtypeStruct((md, seg), SEGMENT_DTYPE[k]) for k in stored],
        scratch_shapes=[pltpu.VMEM((tm + md, d), BF16)] + ([pltpu.VMEM((tm + md, seg), F32)] if has_glu else []),
        compiler_params=pltpu.CompilerParams(
            dimension_semantics=("arbitrary", "arbitrary"), vmem_limit_bytes=VMEM_LIMIT),
        name="inproj_" + kinds[0],
    )(x2d, xd2d, g_norm.reshape(1, d), w_in_bf, *tab_p, *tab_d)


class _DecodeHost:
    def __init__(self, pt_ref, lam, gs, qd_ref, knd_ref, vnd_ref, sad_ref, ck_hbm, cv_hbm, od_ref,
                 kbuf, vbuf, sems, m_scr, l_scr, acca_scr, accb_scr,
                 *, lam_init, layer, tok0, step, n_steps):
        self.pt_ref, self.lam, self.gs, self.lam_init, self.layer = pt_ref, lam, gs, lam_init, layer
        self.knd_ref, self.vnd_ref, self.sad_ref, self.od_ref = knd_ref, vnd_ref, sad_ref, od_ref
        self.ck_hbm, self.cv_hbm, self.kbuf, self.vbuf, self.sems = ck_hbm, cv_hbm, kbuf, vbuf, sems
        self.state = (m_scr, l_scr, acca_scr, accb_scr)
        self.tok0, self.step, self.n_steps = tok0, step, n_steps
        self.n_tok, self.nh, self.hw = qd_ref.shape
        self.n_parts, self.part = kbuf.shape[0], kbuf.shape[1]
        self.per_tok = self.n_parts * self.part // self.n_tok
        assert self.per_tok * self.n_tok == self.n_parts * self.part
        assert self.per_tok * n_steps == pt_ref.shape[1]
        hw = self.hw
        d_map = lax.broadcasted_iota(jnp.int32, (hw, hw), 0) // HEAD_DIM
        n_map = lax.broadcasted_iota(jnp.int32, (hw, hw), 1) // HEAD_DIM
        self.sel = (d_map == n_map).astype(BF16)
        self.qd = qd_ref[...].astype(F32)

    def _page_copies(self, step_idx, hb, i):
        j = hb * self.part + i
        pid = self.pt_ref[self.tok0 + j // self.per_tok, step_idx * self.per_tok + j % self.per_tok]
        return (pltpu.make_async_copy(self.ck_hbm.at[self.layer, pid], self.kbuf.at[hb, i], self.sems.at[0, hb, i]),
                pltpu.make_async_copy(self.cv_hbm.at[self.layer, pid], self.vbuf.at[hb, i], self.sems.at[1, hb, i]))

    def _start_part(self, step_idx, hb):
        for i in range(self.part):
            for copy in self._page_copies(step_idx, hb, i):
                copy.start()

    def wait_part(self, hb):
        for i in range(self.part):
            for copy in self._page_copies(self.step, hb, i):
                copy.wait()

    def refill_part(self, hb):
        @pl.when(self.step + 1 < self.n_steps)
        def _():
            self._start_part(self.step + 1, hb)

    def begin(self):
        @pl.when(self.step == 0)
        def _():
            for hb in range(self.n_parts):
                self._start_part(0, hb)
            m_scr, l_scr, acca_scr, accb_scr = self.state
            m_scr[...] = jnp.full(m_scr.shape, NEG_INF, F32)
            l_scr[...] = jnp.zeros(l_scr.shape, F32)
            acca_scr[...] = jnp.zeros(acca_scr.shape, F32)
            accb_scr[...] = jnp.zeros(accb_scr.shape, F32)

    @staticmethod
    def _swap_maps(x):
        return pltpu.roll(x, HEAD_DIM, x.ndim - 1)

    @staticmethod
    def _reduce(fn, x):
        ways = DECODE_REDUCE_WAYS if x.shape[0] % DECODE_REDUCE_WAYS == 0 else 1
        return fn(fn(x.reshape(ways, x.shape[0] // ways, *x.shape[1:]), axis=1), axis=0)

    def _scores(self, tok, kp):
        rows = kp.shape[0]
        prod = (kp * self.qd[tok][None]).reshape(rows * self.nh, self.hw)
        return jnp.dot(prod.astype(BF16), self.sel, preferred_element_type=F32)

    def _update(self, t, s, vp):
        m_scr, l_scr, acca_scr, accb_scr = self.state
        nh, hw = self.nh, self.hw
        rows = vp.shape[0]
        m_old = m_scr[t]
        m_new = jnp.maximum(m_old, self._reduce(jnp.max, s.reshape(rows, nh, hw)))
        alpha = jnp.exp2(m_old - m_new)
        p2 = jnp.exp2(s.reshape(rows, nh, hw) - m_new[None]).reshape(rows * nh, hw)
        p = p2.reshape(rows, nh, hw)
        p_sw = self._swap_maps(p2).reshape(rows, nh, hw)
        l_scr[t] = alpha * l_scr[t] + self._reduce(jnp.sum, p)
        acca_scr[t] = alpha * acca_scr[t] + self._reduce(jnp.sum, p * vp)
        accb_scr[t] = self._swap_maps(alpha) * accb_scr[t] + self._reduce(jnp.sum, p_sw * vp)
        m_scr[t] = m_new

    def page_scores(self, hb, i):
        return self._scores((hb * self.part + i) // self.per_tok, self.kbuf[hb, i])

    def page_update(self, hb, i, s):
        self._update(hb * self.part + i, s, self.vbuf[hb, i])

    def finish(self):
        @pl.when(self.step == self.n_steps - 1)
        def _():
            m_scr, l_scr, acca_scr, accb_scr = self.state
            nh, hw, per_tok = self.nh, self.hw, self.per_tok
            lo = lax.broadcasted_iota(jnp.int32, (nh, hw), 1) < HEAD_DIM
            for tok in range(self.n_tok):
                t0 = tok * per_tok
                self._update(t0, self._scores(tok, self.knd_ref[tok][None]), self.vnd_ref[tok][None])
                m_all = m_scr[t0:t0 + per_tok]
                m = jnp.max(m_all, axis=0)
                w = jnp.exp2(m_all - m[None])
                w_sw = self._swap_maps(w.reshape(per_tok * nh, hw)).reshape(per_tok, nh, hw)
                l = jnp.sum(w * l_scr[t0:t0 + per_tok], axis=0)
                acc_a = jnp.sum(w * acca_scr[t0:t0 + per_tok], axis=0)
                acc_b = jnp.sum(w_sw * accb_scr[t0:t0 + per_tok], axis=0)
                na = acc_a / l
                nb = acc_b / self._swap_maps(l)
                o = jnp.where(lo, na, nb) - self.lam * jnp.where(lo, nb, na)
                o = _rms(o, self.gs) * (1.0 - self.lam_init)
                self.od_ref[tok] = o * self.sad_ref[tok].astype(F32)


def _decode_host_specs(n_tok, tok0, nh, hw, page, pages_per_step):
    assert tok0 % n_tok == 0 and pages_per_step % N_HALVES == 0
    tok_spec = pl.BlockSpec((n_tok, nh, hw), lambda *_: (tok0 // n_tok, 0, 0))
    part = pages_per_step // N_HALVES
    scratch = [
        pltpu.VMEM((N_HALVES, part, page, nh, hw), F32), pltpu.VMEM((N_HALVES, part, page, nh, hw), F32),
        pltpu.SemaphoreType.DMA((2, N_HALVES, part)),
    ] + [pltpu.VMEM((pages_per_step, nh, hw), F32)] * 4
    return tok_spec, pl.BlockSpec(memory_space=pl.ANY), scratch


def _attn_kernel(pt_ref, lq1, lk1, lq2, lk2, gs_ref, q_ref, k_ref, v_ref, sa_ref,
                 qd_ref, knd_ref, vnd_ref, sad_ref, ck_hbm, cv_hbm, o_ref, od_ref,
                 kt_scr, vb_scr, kbuf, vbuf, sems, m_scr, l_scr, acca_scr, accb_scr,
                 *, tq, lam_init, layer, tok0, n_steps):
    s_len, hw = k_ref.shape
    step = pl.program_id(0) * pl.num_programs(1) + pl.program_id(1)
    lam = _lam(lq1, lk1, lq2, lk2, lam_init)
    gs = gs_ref[...]
    dec = _DecodeHost(pt_ref, lam, gs, qd_ref, knd_ref, vnd_ref, sad_ref, ck_hbm, cv_hbm, od_ref,
                      kbuf, vbuf, sems, m_scr, l_scr, acca_scr, accb_scr,
                      lam_init=lam_init, layer=layer, tok0=tok0, step=step, n_steps=n_steps)
    dec.begin()

    kt_scr[...] = k_ref[...].T.astype(BF16)
    vb_scr[:, 0:hw] = v_ref[...].astype(BF16)
    vb_scr[:, hw:2 * hw] = jnp.ones((s_len, hw), BF16)
    lane = lax.broadcasted_iota(jnp.int32, (tq, hw), 1)
    q_row = lax.broadcasted_iota(jnp.int32, (2 * tq, tq), 0) % tq
    k_col = lax.broadcasted_iota(jnp.int32, (2 * tq, tq), 1)
    causal = k_col <= q_row

    def attention_tile(i):
        r0, c_diag = i * tq, i * tq
        q = q_ref[r0:r0 + tq, :]
        zero = jnp.zeros_like(q)
        q2 = jnp.concatenate(
            [jnp.where(lane < HEAD_DIM, q, zero), jnp.where(lane >= HEAD_DIM, q, zero)], axis=0)
        s = jnp.dot(q2, kt_scr[:, 0:c_diag + tq], preferred_element_type=F32)
        s_diag = jnp.where(causal, s[:, c_diag:], NEG_INF)
        m = jnp.max(s_diag, axis=-1, keepdims=True)
        if i > 0:
            m = jnp.maximum(m, jnp.max(s[:, 0:c_diag], axis=-1, keepdims=True))
        p = jnp.exp2(s_diag - m).astype(BF16)
        if i > 0:
            p = jnp.concatenate([jnp.exp2(s[:, 0:c_diag] - m).astype(BF16), p], axis=1)
        acc = jnp.dot(p, vb_scr[0:c_diag + tq, :], preferred_element_type=F32)
        o = acc[0:tq, 0:hw] / acc[0:tq, hw:] - lam * (acc[tq:, 0:hw] / acc[tq:, hw:])
        o = _rms(o, gs) * (1.0 - lam_init)
        o_ref[r0:r0 + tq, :] = (o * sa_ref[r0:r0 + tq, :].astype(F32)).astype(o_ref.dtype)

    def part_of_step(q_tiles, hb):
        work = sum(i + 1 for i in q_tiles)
        work_done = pages_done = 0
        for i in q_tiles:
            attention_tile(i)
            work_done += i + 1
            while pages_done < work_done * dec.part // work:
                dec.page_update(hb, pages_done, dec.page_scores(hb, pages_done))
                pages_done += 1

    n_q = s_len // tq
    tiles_upto = lambda n: n * (n + 1) // 2
    split = min(range(n_q + 1), key=lambda n: abs(2 * tiles_upto(n) - tiles_upto(n_q)))
    for hb, q_tiles in enumerate((range(0, split), range(split, n_q))):
        dec.wait_part(hb)
        part_of_step(q_tiles, hb)
        dec.refill_part(hb)
    dec.finish()


def _attention(q, k, v, sa, dec_ops, caches, page_table, layer, tok0, n_tok, lams, g_subln, tq, lam_init):
    b, s, aw = q.shape
    _, nh, hw = dec_ops[0].shape
    n_pages, page = page_table.shape[1], caches[0].shape[2]
    n_steps = b * nh
    assert aw == nh * hw and (n_tok * n_pages) % n_steps == 0
    pages_per_step = n_tok * n_pages // n_steps
    head = lambda bi, h, pt: (bi, 0, h)
    small = lambda bi, h, pt: (0, 0)
    head_spec = pl.BlockSpec((None, s, hw), head)
    tok_spec, hbm_spec, dec_scratch = _decode_host_specs(n_tok, tok0, nh, hw, page, pages_per_step)
    grid_spec = pltpu.PrefetchScalarGridSpec(
        num_scalar_prefetch=1,
        grid=(b, nh),
        in_specs=[pl.BlockSpec((1, HEAD_DIM), small)] * 4 + [pl.BlockSpec((1, hw), small)]
        + [head_spec] * 4 + [tok_spec] * 4 + [hbm_spec] * 2,
        out_specs=[head_spec, pl.BlockSpec((n_tok, nh, hw), lambda bi, h, pt: (0, 0, 0))],
        scratch_shapes=[pltpu.VMEM((hw, s), BF16), pltpu.VMEM((s, 2 * hw), BF16)] + dec_scratch,
    )
    return pl.pallas_call(
        functools.partial(_attn_kernel, tq=tq, lam_init=lam_init, layer=layer, tok0=tok0, n_steps=n_steps),
        grid_spec=grid_spec,
        out_shape=[jax.ShapeDtypeStruct((b, s, aw), BF16), jax.ShapeDtypeStruct((n_tok, nh, hw), F32)],
        compiler_params=pltpu.CompilerParams(
            dimension_semantics=("arbitrary", "arbitrary"), vmem_limit_bytes=VMEM_LIMIT),
        name="attention",
    )(page_table, *lams, g_subln.reshape(1, hw), q, k, v, sa, *dec_ops, *caches)


def _conv_window(gc_ref, gh_ref, sh_scr, first_tile):
    ts, halo = gc_ref.shape[0], gh_ref.shape[0]
    n_copy = halo + ts - SUBLANES
    sh_scr[0, 0:halo, :] = jnp.where(first_tile, 0.0, gh_ref[...])
    sh_scr[0, halo:halo + ts, :] = gc_ref[...]
    for ph in range(1, SUBLANES):
        sh_scr[ph, 0:n_copy, :] = sh_scr[0, ph:ph + n_copy, :]


def _conv_rows(s0, wdw_ref, bdw_ref, gcl_ref, bcl_ref, wpw_ref, sc_ref, sh_scr, c_scr):
    kw, cw = wdw_ref.shape
    halo = sh_scr.shape[1] - sc_ref.shape[0]
    off = halo - (kw - 1)
    rc, lc = CONV_ROW_CHUNK, CONV_LANE_CHUNK
    for c0 in range(0, cw, lc):
        for r0 in range(s0, s0 + CONV_SUBTILE, rc):
            acc = jnp.broadcast_to(bdw_ref[:, c0:c0 + lc], (rc, lc))
            for j in range(kw):
                ph, base = (off + j) % SUBLANES, (off + j) // SUBLANES * SUBLANES
                acc = acc + wdw_ref[j:j + 1, c0:c0 + lc] * sh_scr[ph, r0 + base:r0 + base + rc, c0:c0 + lc]
            c_scr[r0:r0 + rc, c0:c0 + lc] = acc
    rows = slice(s0, s0 + CONV_SUBTILE)
    c = c_scr[rows, :]
    mu = jnp.mean(c, axis=-1, keepdims=True)
    var = jnp.mean(jnp.square(c - mu), axis=-1, keepdims=True)
    y = (c - mu) * lax.rsqrt(var + EPS) * gcl_ref[...] + bcl_ref[...]
    pw = jnp.dot(_silu(y).astype(BF16), wpw_ref[...], preferred_element_type=F32)
    return (pw * sc_ref[rows, :].astype(F32)).astype(BF16)


def _decode_conv_kernel(hist_ref, wdw_ref, bdw_ref, gcl_ref, bcl_ref, wpw_ref, sc_ref, o_ref):
    kw = hist_ref.shape[0]
    acc = jnp.broadcast_to(bdw_ref[...], o_ref.shape)
    for j in range(kw):
        acc = acc + wdw_ref[j:j + 1, :] * hist_ref[j]
    mu = jnp.mean(acc, axis=-1, keepdims=True)
    var = jnp.mean(jnp.square(acc - mu), axis=-1, keepdims=True)
    y = (acc - mu) * lax.rsqrt(var + EPS) * gcl_ref[...] + bcl_ref[...]
    c = jnp.dot(_silu(y).astype(BF16), wpw_ref[...], preferred_element_type=F32)
    o_ref[...] = (c * sc_ref[...].astype(F32)).astype(o_ref.dtype)


def _decode_conv(hist_t, sc, w_dw, b_dw, g_cln, b_cln, w_pw_bf):
    kw, db, cw = hist_t.shape
    vec = lambda a: a.reshape(1, cw)
    return pl.pallas_call(
        _decode_conv_kernel,
        out_shape=jax.ShapeDtypeStruct((db, cw), BF16),
        compiler_params=pltpu.CompilerParams(vmem_limit_bytes=VMEM_LIMIT),
        name="decode_conv",
    )(hist_t, w_dw, vec(b_dw), vec(g_cln), vec(b_cln), w_pw_bf, sc)


def _out_rows(x, oa, oc, wout_ref):
    aw = oa.shape[1]
    h = x + jnp.dot(oa, wout_ref[0:aw, :], preferred_element_type=F32)
    return h + jnp.dot(oc, wout_ref[aw:, :], preferred_element_type=F32)


def _ple_rows(rows, h, p_ref, gple_ref, wpg_ref, wple_ref, gfin_ref, y_ref, final_norm):
    r = _rms(h, gple_ref[...]).astype(BF16)
    gate = _sigmoid(jnp.dot(r, wpg_ref[...], preferred_element_type=F32))
    e = jnp.dot(p_ref[rows, :].astype(BF16), wple_ref[...], preferred_element_type=F32)
    h = h + gate * e
    y_ref[rows, :] = _rms(h, gfin_ref[...]) if final_norm else h


def _out_kernel(x_ref, oa_ref, oc_ref, p_ref, wout_ref, gple_ref, wpg_ref, wple_ref, gfin_ref, y_ref,
                *, final_norm):
    tm = oa_ref.shape[0]
    sub = min(tm, OUT_SUBTILE)
    for r0 in range(0, tm, sub):
        rows = slice(r0, r0 + sub)
        h = _out_rows(x_ref[rows, :], oa_ref[rows, :], oc_ref[rows, :], wout_ref)
        _ple_rows(rows, h, p_ref, gple_ref, wpg_ref, wple_ref, gfin_ref, y_ref, final_norm)


def _out_conv_kernel(x_ref, oa_ref, p_ref, wout_ref, gple_ref, wpg_ref, wple_ref, gfin_ref,
                     g0_ref, sc0_ref, gc_ref, gh_ref, sc_ref, wdw_ref, bdw_ref, gcl_ref, bcl_ref, wpw_ref,
                     y_ref, sh_scr, c_scr, oc_scr, *, final_norm, n_steps, tiles_per_seq):
    tm = oa_ref.shape[0]
    step = pl.program_id(0)
    conv_weights = (wdw_ref, bdw_ref, gcl_ref, bcl_ref, wpw_ref)
    sub_tiles = list(range(0, tm, CONV_SUBTILE))

    @pl.when(step == 0)
    def _first_conv():
        _conv_window(g0_ref, gh_ref, sh_scr, True)
        for s0 in sub_tiles:
            oc_scr[0, s0:s0 + CONV_SUBTILE, :] = _conv_rows(s0, *conv_weights, sc0_ref, sh_scr, c_scr)

    nxt = jnp.minimum(step + 1, n_steps - 1)
    h = _out_rows(x_ref[...], oa_ref[...], oc_scr[step % 2], wout_ref)
    _conv_window(gc_ref, gh_ref, sh_scr, nxt % tiles_per_seq == 0)
    for s0 in sub_tiles:
        oc_scr[(step + 1) % 2, s0:s0 + CONV_SUBTILE, :] = _conv_rows(s0, *conv_weights, sc_ref, sh_scr, c_scr)
    _ple_rows(slice(0, tm), h, p_ref, gple_ref, wpg_ref, wple_ref, gfin_ref, y_ref, final_norm)


def _outproj(x2d, oa, oc, p2d, w_out_bf, g_ple, w_pg_bf, w_ple_bf, g_final, tm, final_norm):
    m, d = x2d.shape
    aw, cw, pd = oa.shape[1], oc.shape[1], p2d.shape[1]
    row = lambda i: (i, 0)
    full = lambda i: (0, 0)
    once = pl.Buffered(1)
    return pl.pallas_call(
        functools.partial(_out_kernel, final_norm=final_norm),
        grid=(m // tm,),
        in_specs=[
            pl.BlockSpec((tm, d), row),
            pl.BlockSpec((tm, aw), row),
            pl.BlockSpec((tm, cw), row),
            pl.BlockSpec((tm, pd), row),
            pl.BlockSpec((aw + cw, d), full, pipeline_mode=once),
            pl.BlockSpec((1, d), full),
            pl.BlockSpec((d, d), full, pipeline_mode=once),
            pl.BlockSpec((pd, d), full, pipeline_mode=once),
            pl.BlockSpec((1, d), full),
        ],
        out_specs=pl.BlockSpec((tm, d), row),
        out_shape=jax.ShapeDtypeStruct((m, d), F32),
        compiler_params=pltpu.CompilerParams(
            dimension_semantics=("parallel",), vmem_limit_bytes=VMEM_LIMIT),
        name="outproj",
    )(x2d, oa, oc, p2d, w_out_bf, g_ple.reshape(1, d), w_pg_bf, w_ple_bf, g_final.reshape(1, d))


def _prompt_outproj(x2d, oa, g2d, sc2d, p2d, conv_params, w_out_bf, g_ple, w_pg_bf, w_ple_bf, g_final,
                    tm, seq_len, final_norm):
    m, d = x2d.shape
    aw, cw, pd = oa.shape[1], g2d.shape[1], p2d.shape[1]
    w_dw, b_dw, g_cln, b_cln, w_pw_bf = conv_params
    kw = w_dw.shape[0]
    halo = -(-(kw - 1) // SUBLANES) * SUBLANES
    assert tm % halo == 0 and seq_len % tm == 0 and tm % CONV_SUBTILE == 0
    hb = tm // halo
    n_steps = m // tm
    row = lambda i: (i, 0)
    full = lambda i: (0, 0)
    ahead = lambda i: (jnp.minimum(i + 1, n_steps - 1), 0)
    once = pl.Buffered(1)
    vec = lambda a, n: a.reshape(1, n)
    return pl.pallas_call(
        functools.partial(_out_conv_kernel, final_norm=final_norm, n_steps=n_steps, tiles_per_seq=seq_len // tm),
        grid=(n_steps,),
        in_specs=[
            pl.BlockSpec((tm, d), row),
            pl.BlockSpec((tm, aw), row),
            pl.BlockSpec((tm, pd), row),
            pl.BlockSpec((aw + cw, d), full, pipeline_mode=once),
            pl.BlockSpec((1, d), full),
            pl.BlockSpec((d, d), full, pipeline_mode=once),
            pl.BlockSpec((pd, d), full, pipeline_mode=once),
            pl.BlockSpec((1, d), full),
            pl.BlockSpec((tm, cw), full),
            pl.BlockSpec((tm, cw), full),
            pl.BlockSpec((tm, cw), ahead),
            pl.BlockSpec((halo, cw), lambda i: (jnp.maximum(ahead(i)[0] * hb - 1, 0), 0)),
            pl.BlockSpec((tm, cw), ahead),
            pl.BlockSpec((kw, cw), full),
            pl.BlockSpec((1, cw), full),
            pl.BlockSpec((1, cw), full),
            pl.BlockSpec((1, cw), full),
            pl.BlockSpec((cw, cw), full, pipeline_mode=once),
        ],
        out_specs=pl.BlockSpec((tm, d), row),
        out_shape=jax.ShapeDtypeStruct((m, d), F32),
        scratch_shapes=[pltpu.VMEM((SUBLANES, halo + tm, cw), F32), pltpu.VMEM((tm, cw), F32),
                        pltpu.VMEM((2, tm, cw), BF16)],
        compiler_params=pltpu.CompilerParams(
            dimension_semantics=("arbitrary",), vmem_limit_bytes=VMEM_LIMIT),
        name="prompt_outproj",
    )(x2d, oa, p2d, w_out_bf, vec(g_ple, d), w_pg_bf, w_ple_bf, vec(g_final, d),
      g2d, sc2d, g2d, g2d, sc2d, w_dw, vec(b_dw, cw), vec(g_cln, cw), vec(b_cln, cw), w_pw_bf)


def kernel(x_prompt, x_sample, p_prompt, p_sample, cache_k, cache_v, state_conv, page_table, w_in, g_norm, lam_q1, lam_k1, lam_q2, lam_k2, g_subln, w_dw, b_dw, g_cln, b_cln, w_pw, w_out, g_ple, w_pg, w_ple, g_final):
    b, s, d = x_prompt.shape
    db, ds, _ = x_sample.shape
    depth = w_in.shape[0]
    nh, hw = cache_k.shape[3], cache_k.shape[4]
    aw = nh * hw
    cw = w_dw.shape[2]
    kw = w_dw.shape[1]
    past = page_table.shape[1] * cache_k.shape[2]
    assert ds == 1 and aw == cw and hw == 2 * HEAD_DIM

    tab_p = _rope_tables(jnp.arange(s, dtype=jnp.int32))
    tab_s = _rope_tables(jnp.full((db * ds,), past, dtype=jnp.int32))

    h_p = x_prompt.reshape(b * s, d)
    h_s = x_sample.reshape(db * ds, d)
    outs = {k: [] for k in ("kp", "vp", "cp", "ks", "vs", "cs")}
    for i in range(depth):
        lam_init = _lam_init(i)
        last = i == depth - 1
        lams = tuple(a[i].reshape(1, HEAD_DIM) for a in (lam_q1, lam_k1, lam_q2, lam_k2))
        w_in_bf, w_pw_bf = w_in[i].astype(BF16), w_pw[i].astype(BF16)
        w_out_bf, w_pg_bf, w_ple_bf = w_out[i].astype(BF16), w_pg[i].astype(BF16), w_ple[i].astype(BF16)

        q, k, v, qd, kd, vd = _inproj(h_p, h_s, g_norm[i], w_in_bf, tab_p, tab_s, tm=1024, seg=aw,
                                      kinds=("q", "k", "v"))
        sa, g, sc, sad, gd, scd = _inproj(h_p, h_s, g_norm[i], w_in_bf, tab_p, tab_s, tm=1024, seg=aw,
                                          kinds=("gate_a", "glu_a", "glu_b", "gate_c"))
        as3 = lambda a: a.reshape(b, s, -1)
        dec_ops = tuple(a.reshape(db, nh, hw) for a in (qd, kd, vd, sad))
        caches = (cache_k, cache_v)

        oa, oad = _attention(as3(q), as3(k), as3(v), as3(sa), dec_ops, caches, page_table, i, 0, db,
                             lams, g_subln[i], tq=256, lam_init=lam_init)
        h_p = _prompt_outproj(h_p, oa.reshape(b * s, aw), g, sc, p_prompt[i].reshape(b * s, -1),
                              (w_dw[i], b_dw[i], g_cln[i], b_cln[i], w_pw_bf), w_out_bf, g_ple[i], w_pg_bf,
                              w_ple_bf, g_final, tm=256, seq_len=s, final_norm=last)
        outs["kp"].append(k.reshape(b, s, nh, hw))
        outs["vp"].append(v.reshape(b, s, nh, hw))
        outs["cp"].append(as3(g)[:, s - (kw - 1):])

        hist = jnp.concatenate([state_conv[i], gd.reshape(db, ds, cw)], axis=1)
        ocd = _decode_conv(hist.transpose(1, 0, 2), scd, w_dw[i], b_dw[i], g_cln[i], b_cln[i], w_pw_bf)
        h_s = _outproj(h_s, oad.reshape(db, aw).astype(BF16), ocd, p_sample[i].reshape(db * ds, -1),
                       w_out_bf, g_ple[i], w_pg_bf, w_ple_bf, g_final, tm=db * ds, final_norm=last)
        outs["ks"].append(kd.reshape(db, ds, nh, hw))
        outs["vs"].append(vd.reshape(db, ds, nh, hw))
        outs["cs"].append(hist[:, 1:])

    st = lambda key: jnp.stack(outs[key])
    return (h_p.reshape(b, s, d), h_s.reshape(db, ds, d),
            st("kp"), st("vp"), st("cp"), st("ks"), st("vs"), st("cs"))
```

```python
import functools
import math

import jax
import jax.numpy as jnp
from jax import lax
from jax.experimental import pallas as pl
from jax.experimental.pallas import tpu as pltpu

F32 = jnp.float32
BF16 = jnp.bfloat16

HEAD_DIM = 64
ROT_DIM = HEAD_DIM // 4
HALF_ROT = ROT_DIM // 2
ROPE_THETA = 500000.0
EPS = 1e-6
NEG_INF = -1e30
Q_SCALE = HEAD_DIM ** -0.5 * math.log2(math.e)

LANES = 128
SUBLANES = 8
VMEM_LIMIT = 60 * 1024 * 1024
MXU_DIM = 256
INPROJ_CHUNK = MXU_DIM
OUT_SUBTILE = 256
CONV_SUBTILE = 128
CONV_ROW_CHUNK = 64
CONV_LANE_CHUNK = 2 * LANES
N_HALVES = 2
DECODE_REDUCE_WAYS = 4


def _sigmoid(x):
    return 1.0 / (1.0 + jnp.exp(-x))


def _silu(x):
    return x * _sigmoid(x)


def _rms(x, g):
    return x * lax.rsqrt(jnp.mean(x * x, axis=-1, keepdims=True) + EPS) * g


def _lam_init(layer_idx):
    return 0.8 - 0.6 * math.exp(-0.3 * layer_idx)


def _lam(lq1, lk1, lq2, lk2, lam_init):
    a = jnp.sum(lq1[...] * lk1[...], keepdims=True)
    b = jnp.sum(lq2[...] * lk2[...], keepdims=True)
    return jnp.exp(a) - jnp.exp(b) + lam_init


def _rope_tables(pos):
    inv = ROPE_THETA ** (-jnp.arange(0, ROT_DIM, 2, dtype=F32) / ROT_DIM)
    ang = pos.astype(F32)[:, None] * inv[None, :]
    d = jnp.arange(2 * HEAD_DIM) % HEAD_DIM
    f = d % HALF_ROT
    cos_f = jnp.cos(ang)[:, f]
    sin_f = jnp.sin(ang)[:, f]
    lo = (d < HALF_ROT)[None, :]
    hi = ((d >= HALF_ROT) & (d < ROT_DIM))[None, :]
    cos_t = jnp.where(lo | hi, cos_f, 1.0)
    s_next = jnp.where(lo, -sin_f, 0.0)
    s_prev = jnp.where(hi, sin_f, 0.0)
    return cos_t, s_next, s_prev


SEGMENTS = ("q", "k", "v", "gate_a", "glu_a", "glu_b", "gate_c")
SEGMENT_DTYPE = {"q": BF16, "k": F32, "v": F32, "gate_a": BF16, "glu_b": F32, "gate_c": BF16}


def _inproj_kernel(*refs, kinds):
    (x_ref, xd_ref, gn_ref, w_ref, cos_ref, snext_ref, sprev_ref, cosd_ref, snextd_ref, sprevd_ref), refs = refs[:10], refs[10:]
    stored = [k for k in kinds if k in SEGMENT_DTYPE]
    out_p = dict(zip(stored, refs[:len(stored)]))
    out_d = dict(zip(stored, refs[len(stored):2 * len(stored)]))
    u_scr, *glu_scr = refs[2 * len(stored):]
    n = pl.program_id(1)
    tm, seg = x_ref.shape[0], w_ref.shape[1]

    def segment(epilogue):
        u = u_scr[...]
        for c0 in range(0, seg, INPROJ_CHUNK):
            cols = slice(c0, c0 + INPROJ_CHUNK)
            z = jnp.dot(u, w_ref[:, cols], preferred_element_type=F32)
            epilogue(z, cols)

    def rope(z, scale):
        tables = [jnp.concatenate([t[...], td[...]], axis=0)
                  for t, td in ((cos_ref, cosd_ref), (snext_ref, snextd_ref), (sprev_ref, sprevd_ref))]
        cos_t, s_next, s_prev = tables
        heads = []
        for h in range(z.shape[1] // LANES):
            zh = z[:, h * LANES:(h + 1) * LANES]
            nxt = pltpu.roll(zh, LANES - HALF_ROT, 1)
            prv = pltpu.roll(zh, HALF_ROT, 1)
            heads.append((zh * cos_t + nxt * s_next + prv * s_prev) * scale)
        return jnp.concatenate(heads, axis=1)

    def store(kind, fn):
        def epilogue(z, cols):
            r = fn(z, cols)
            out_p[kind][:, cols] = r[0:tm].astype(out_p[kind].dtype)
            out_d[kind][:, cols] = r[tm:].astype(out_d[kind].dtype)
        return epilogue

    epilogues = {
        "q": store("q", lambda z, cols: rope(z, Q_SCALE)),
        "k": store("k", lambda z, cols: rope(z, 1.0)),
        "v": store("v", lambda z, cols: z),
        "gate_a": store("gate_a", lambda z, cols: _silu(z)),
        "gate_c": store("gate_c", lambda z, cols: _silu(z)),
        "glu_b": store("glu_b", lambda z, cols: glu_scr[0][:, cols] * _sigmoid(z)),
    }

    def keep_glu_a(z, cols):
        glu_scr[0][:, cols] = z

    for idx, kind in enumerate(kinds):
        @pl.when(n == idx)
        def _(idx=idx, kind=kind):
            if idx == 0:
                u_scr[0:tm, :] = _rms(x_ref[...], gn_ref[...]).astype(BF16)
                u_scr[tm:, :] = _rms(xd_ref[...], gn_ref[...]).astype(BF16)
            segment(keep_glu_a if kind == "glu_a" else epilogues[kind])


def _inproj(x2d, xd2d, g_norm, w_in_bf, tab_p, tab_d, tm, seg, kinds):
    m, d = x2d.shape
    md = xd2d.shape[0]
    seg0 = SEGMENTS.index(kinds[0])
    assert SEGMENTS[seg0:seg0 + len(kinds)] == tuple(kinds) and m % tm == 0 and w_in_bf.shape[1] == seg * len(SEGMENTS)
    t_blocks = tab_p[0].shape[0] // tm
    row = lambda i, n: (i, 0)
    fixed = lambda i, n: (0, 0)
    tab = lambda i, n: (i % t_blocks, 0)
    stored = [k for k in kinds if k in SEGMENT_DTYPE]
    has_glu = "glu_a" in kinds
    assert has_glu == ("glu_b" in kinds)
    return pl.pallas_call(
        functools.partial(_inproj_kernel, kinds=tuple(kinds)),
        grid=(m // tm, len(kinds)),
        in_specs=[
            pl.BlockSpec((tm, d), row),
            pl.BlockSpec((md, d), fixed),
            pl.BlockSpec((1, d), fixed),
            pl.BlockSpec((d, seg), lambda i, n: (0, seg0 + n)),
        ] + [pl.BlockSpec((tm, LANES), tab)] * 3 + [pl.BlockSpec((md, LANES), fixed)] * 3,
        out_specs=[pl.BlockSpec((tm, seg), row)] * len(stored) + [pl.BlockSpec((md, seg), fixed)] * len(stored),
        out_shape=[jax.ShapeDtypeStruct((m, seg), SEGMENT_DTYPE[k]) for k in stored]
        + [jax.ShapeDtypeStruct((md, seg), SEGMENT_DTYPE[k]) for k in stored],
        scratch_shapes=[pltpu.VMEM((tm + md, d), BF16)] + ([pltpu.VMEM((tm + md, seg), F32)] if has_glu else []),
        compiler_params=pltpu.CompilerParams(
            dimension_semantics=("arbitrary", "arbitrary"), vmem_limit_bytes=VMEM_LIMIT),
        name="inproj_" + kinds[0],
    )(x2d, xd2d, g_norm.reshape(1, d), w_in_bf, *tab_p, *tab_d)


class _DecodeHost:
    def __init__(self, pt_ref, lam, gs, qd_ref, knd_ref, vnd_ref, sad_ref, ck_hbm, cv_hbm, od_ref,
                 kbuf, vbuf, sems, m_scr, l_scr, acca_scr, accb_scr,
                 *, lam_init, layer, tok0, step, n_steps):
        self.pt_ref, self.lam, self.gs, self.lam_init, self.layer = pt_ref, lam, gs, lam_init, layer
        self.knd_ref, self.vnd_ref, self.sad_ref, self.od_ref = knd_ref, vnd_ref, sad_ref, od_ref
        self.ck_hbm, self.cv_hbm, self.kbuf, self.vbuf, self.sems = ck_hbm, cv_hbm, kbuf, vbuf, sems
        self.state = (m_scr, l_scr, acca_scr, accb_scr)
        self.tok0, self.step, self.n_steps = tok0, step, n_steps
        self.n_tok, self.nh, self.hw = qd_ref.shape
        self.n_parts, self.part = kbuf.shape[0], kbuf.shape[1]
        self.per_tok = self.n_parts * self.part // self.n_tok
        assert self.per_tok * self.n_tok == self.n_parts * self.part
        assert self.per_tok * n_steps == pt_ref.shape[1]
        hw = self.hw
        d_map = lax.broadcasted_iota(jnp.int32, (hw, hw), 0) // HEAD_DIM
        n_map = lax.broadcasted_iota(jnp.int32, (hw, hw), 1) // HEAD_DIM
        self.sel = (d_map == n_map).astype(BF16)
        self.qd = qd_ref[...].astype(F32)

    def _page_copies(self, step_idx, hb, i):
        j = hb * self.part + i
        pid = self.pt_ref[self.tok0 + j // self.per_tok, step_idx * self.per_tok + j % self.per_tok]
        return (pltpu.make_async_copy(self.ck_hbm.at[self.layer, pid], self.kbuf.at[hb, i], self.sems.at[0, hb, i]),
                pltpu.make_async_copy(self.cv_hbm.at[self.layer, pid], self.vbuf.at[hb, i], self.sems.at[1, hb, i]))

    def _start_part(self, step_idx, hb):
        for i in range(self.part):
            for copy in self._page_copies(step_idx, hb, i):
                copy.start()

    def wait_part(self, hb):
        for i in range(self.part):
            for copy in self._page_copies(self.step, hb, i):
                copy.wait()

    def refill_part(self, hb):
        @pl.when(self.step + 1 < self.n_steps)
        def _():
            self._start_part(self.step + 1, hb)

    def begin(self):
        @pl.when(self.step == 0)
        def _():
            for hb in range(self.n_parts):
                self._start_part(0, hb)
            m_scr, l_scr, acca_scr, accb_scr = self.state
            m_scr[...] = jnp.full(m_scr.shape, NEG_INF, F32)
            l_scr[...] = jnp.zeros(l_scr.shape, F32)
            acca_scr[...] = jnp.zeros(acca_scr.shape, F32)
            accb_scr[...] = jnp.zeros(accb_scr.shape, F32)

    @staticmethod
    def _swap_maps(x):
        return pltpu.roll(x, HEAD_DIM, x.ndim - 1)

    @staticmethod
    def _reduce(fn, x):
        ways = DECODE_REDUCE_WAYS if x.shape[0] % DECODE_REDUCE_WAYS == 0 else 1
        return fn(fn(x.reshape(ways, x.shape[0] // ways, *x.shape[1:]), axis=1), axis=0)

    def _scores(self, tok, kp):
        rows = kp.shape[0]
        prod = (kp * self.qd[tok][None]).reshape(rows * self.nh, self.hw)
        return jnp.dot(prod.astype(BF16), self.sel, preferred_element_type=F32)

    def _update(self, t, s, load_v, rows):
        m_scr, l_scr, acca_scr, accb_scr = self.state
        nh, hw = self.nh, self.hw
        s3 = s.reshape(rows, nh, hw)
        m_old = m_scr[t]
        m_new = jnp.maximum(m_old, self._reduce(jnp.max, s3))
        alpha = jnp.exp2(m_old - m_new)
        chunk = rows // DECODE_REDUCE_WAYS if rows % DECODE_REDUCE_WAYS == 0 else rows
        l, acc_a, acc_b = alpha * l_scr[t], alpha * acca_scr[t], self._swap_maps(alpha) * accb_scr[t]
        for r0 in range(0, rows, chunk):
            p = jnp.exp2(s3[r0:r0 + chunk] - m_new[None])
            p_sw = self._swap_maps(p.reshape(chunk * nh, hw)).reshape(chunk, nh, hw)
            vp = load_v(r0, chunk)
            l = l + jnp.sum(p, axis=0)
            acc_a = acc_a + jnp.sum(p * vp, axis=0)
            acc_b = acc_b + jnp.sum(p_sw * vp, axis=0)
        l_scr[t], acca_scr[t], accb_scr[t] = l, acc_a, acc_b
        m_scr[t] = m_new

    def page_scores(self, hb, i):
        return self._scores((hb * self.part + i) // self.per_tok, self.kbuf[hb, i])

    def page_update(self, hb, i, s):
        self._update(hb * self.part + i, s, lambda r0, n: self.vbuf[hb, i, r0:r0 + n], self.vbuf.shape[2])

    def finish(self):
        @pl.when(self.step == self.n_steps - 1)
        def _():
            m_scr, l_scr, acca_scr, accb_scr = self.state
            nh, hw, per_tok = self.nh, self.hw, self.per_tok
            lo = lax.broadcasted_iota(jnp.int32, (nh, hw), 1) < HEAD_DIM
            for tok in range(self.n_tok):
                t0 = tok * per_tok
                self._update(t0, self._scores(tok, self.knd_ref[tok][None]),
                             lambda r0, n, tok=tok: self.vnd_ref[tok][None], 1)
                m_all = m_scr[t0:t0 + per_tok]
                m = jnp.max(m_all, axis=0)
                w = jnp.exp2(m_all - m[None])
                w_sw = self._swap_maps(w.reshape(per_tok * nh, hw)).reshape(per_tok, nh, hw)
                l = jnp.sum(w * l_scr[t0:t0 + per_tok], axis=0)
                acc_a = jnp.sum(w * acca_scr[t0:t0 + per_tok], axis=0)
                acc_b = jnp.sum(w_sw * accb_scr[t0:t0 + per_tok], axis=0)
                na = acc_a / l
                nb = acc_b / self._swap_maps(l)
                o = jnp.where(lo, na, nb) - self.lam * jnp.where(lo, nb, na)
                o = _rms(o, self.gs) * (1.0 - self.lam_init)
                self.od_ref[tok] = o * self.sad_ref[tok].astype(F32)


def _decode_host_specs(n_tok, tok0, nh, hw, page, pages_per_step):
    assert tok0 % n_tok == 0 and pages_per_step % N_HALVES == 0
    tok_spec = pl.BlockSpec((n_tok, nh, hw), lambda *_: (tok0 // n_tok, 0, 0))
    part = pages_per_step // N_HALVES
    scratch = [
        pltpu.VMEM((N_HALVES, part, page, nh, hw), F32), pltpu.VMEM((N_HALVES, part, page, nh, hw), F32),
        pltpu.SemaphoreType.DMA((2, N_HALVES, part)),
    ] + [pltpu.VMEM((pages_per_step, nh, hw), F32)] * 4
    return tok_spec, pl.BlockSpec(memory_space=pl.ANY), scratch


def _attn_kernel(pt_ref, lq1, lk1, lq2, lk2, gs_ref, q_ref, k_ref, v_ref, sa_ref,
                 qd_ref, knd_ref, vnd_ref, sad_ref, ck_hbm, cv_hbm, o_ref, od_ref,
                 kt_scr, vb_scr, kbuf, vbuf, sems, m_scr, l_scr, acca_scr, accb_scr,
                 *, tq, lam_init, layer, tok0, n_steps):
    s_len, hw = k_ref.shape
    step = pl.program_id(0) * pl.num_programs(1) + pl.program_id(1)
    lam = _lam(lq1, lk1, lq2, lk2, lam_init)
    gs = gs_ref[...]
    dec = _DecodeHost(pt_ref, lam, gs, qd_ref, knd_ref, vnd_ref, sad_ref, ck_hbm, cv_hbm, od_ref,
                      kbuf, vbuf, sems, m_scr, l_scr, acca_scr, accb_scr,
                      lam_init=lam_init, layer=layer, tok0=tok0, step=step, n_steps=n_steps)
    dec.begin()

    kt_scr[...] = k_ref[...].T.astype(BF16)
    vb_scr[:, 0:hw] = v_ref[...].astype(BF16)
    vb_scr[:, hw:2 * hw] = jnp.ones((s_len, hw), BF16)
    lane = lax.broadcasted_iota(jnp.int32, (tq, hw), 1)
    q_row = lax.broadcasted_iota(jnp.int32, (2 * tq, tq), 0) % tq
    k_col = lax.broadcasted_iota(jnp.int32, (2 * tq, tq), 1)
    causal = k_col <= q_row

    def attention_tile(i):
        r0, c_diag = i * tq, i * tq
        q = q_ref[r0:r0 + tq, :]
        zero = jnp.zeros_like(q)
        q2 = jnp.concatenate(
            [jnp.where(lane < HEAD_DIM, q, zero), jnp.where(lane >= HEAD_DIM, q, zero)], axis=0)
        s = jnp.dot(q2, kt_scr[:, 0:c_diag + tq], preferred_element_type=F32)
        s_diag = jnp.where(causal, s[:, c_diag:], NEG_INF)
        m = jnp.max(s_diag, axis=-1, keepdims=True)
        if i > 0:
            m = jnp.maximum(m, jnp.max(s[:, 0:c_diag], axis=-1, keepdims=True))
        p = jnp.exp2(s_diag - m).astype(BF16)
        if i > 0:
            p = jnp.concatenate([jnp.exp2(s[:, 0:c_diag] - m).astype(BF16), p], axis=1)
        acc = jnp.dot(p, vb_scr[0:c_diag + tq, :], preferred_element_type=F32)
        o = acc[0:tq, 0:hw] / acc[0:tq, hw:] - lam * (acc[tq:, 0:hw] / acc[tq:, hw:])
        o = _rms(o, gs) * (1.0 - lam_init)
        o_ref[r0:r0 + tq, :] = (o * sa_ref[r0:r0 + tq, :].astype(F32)).astype(o_ref.dtype)

    def part_of_step(q_tiles, hb):
        work = sum(i + 1 for i in q_tiles)
        work_done = pages_done = 0
        for i in q_tiles:
            attention_tile(i)
            work_done += i + 1
            while pages_done < work_done * dec.part // work:
                dec.page_update(hb, pages_done, dec.page_scores(hb, pages_done))
                pages_done += 1

    n_q = s_len // tq
    tiles_upto = lambda n: n * (n + 1) // 2
    split = min(range(n_q + 1), key=lambda n: abs(2 * tiles_upto(n) - tiles_upto(n_q)))
    for hb, q_tiles in enumerate((range(0, split), range(split, n_q))):
        dec.wait_part(hb)
        part_of_step(q_tiles, hb)
        dec.refill_part(hb)
    dec.finish()


def _attention(q, k, v, sa, dec_ops, caches, page_table, layer, tok0, n_tok, lams, g_subln, tq, lam_init):
    b, s, aw = q.shape
    _, nh, hw = dec_ops[0].shape
    n_pages, page = page_table.shape[1], caches[0].shape[2]
    n_steps = b * nh
    assert aw == nh * hw and (n_tok * n_pages) % n_steps == 0
    pages_per_step = n_tok * n_pages // n_steps
    head = lambda bi, h, pt: (bi, 0, h)
    small = lambda bi, h, pt: (0, 0)
    head_spec = pl.BlockSpec((None, s, hw), head)
    tok_spec, hbm_spec, dec_scratch = _decode_host_specs(n_tok, tok0, nh, hw, page, pages_per_step)
    grid_spec = pltpu.PrefetchScalarGridSpec(
        num_scalar_prefetch=1,
        grid=(b, nh),
        in_specs=[pl.BlockSpec((1, HEAD_DIM), small)] * 4 + [pl.BlockSpec((1, hw), small)]
        + [head_spec] * 4 + [tok_spec] * 4 + [hbm_spec] * 2,
        out_specs=[head_spec, pl.BlockSpec((n_tok, nh, hw), lambda bi, h, pt: (0, 0, 0))],
        scratch_shapes=[pltpu.VMEM((hw, s), BF16), pltpu.VMEM((s, 2 * hw), BF16)] + dec_scratch,
    )
    return pl.pallas_call(
        functools.partial(_attn_kernel, tq=tq, lam_init=lam_init, layer=layer, tok0=tok0, n_steps=n_steps),
        grid_spec=grid_spec,
        out_shape=[jax.ShapeDtypeStruct((b, s, aw), BF16), jax.ShapeDtypeStruct((n_tok, nh, hw), F32)],
        compiler_params=pltpu.CompilerParams(
            dimension_semantics=("arbitrary", "arbitrary"), vmem_limit_bytes=VMEM_LIMIT),
        name="attention",
    )(page_table, *lams, g_subln.reshape(1, hw), q, k, v, sa, *dec_ops, *caches)


def _conv_window(gc_ref, gh_ref, sh_scr, first_tile):
    ts, halo = gc_ref.shape[0], gh_ref.shape[0]
    n_copy = halo + ts - SUBLANES
    sh_scr[0, 0:halo, :] = jnp.where(first_tile, 0.0, gh_ref[...])
    sh_scr[0, halo:halo + ts, :] = gc_ref[...]
    for ph in range(1, SUBLANES):
        sh_scr[ph, 0:n_copy, :] = sh_scr[0, ph:ph + n_copy, :]


def _conv_rows(s0, wdw_ref, bdw_ref, gcl_ref, bcl_ref, wpw_ref, sc_ref, sh_scr, c_scr):
    kw, cw = wdw_ref.shape
    halo = sh_scr.shape[1] - sc_ref.shape[0]
    off = halo - (kw - 1)
    rc, lc = CONV_ROW_CHUNK, CONV_LANE_CHUNK
    for c0 in range(0, cw, lc):
        for r0 in range(s0, s0 + CONV_SUBTILE, rc):
            acc = jnp.broadcast_to(bdw_ref[:, c0:c0 + lc], (rc, lc))
            for j in range(kw):
                ph, base = (off + j) % SUBLANES, (off + j) // SUBLANES * SUBLANES
                acc = acc + wdw_ref[j:j + 1, c0:c0 + lc] * sh_scr[ph, r0 + base:r0 + base + rc, c0:c0 + lc]
            c_scr[r0:r0 + rc, c0:c0 + lc] = acc
    rows = slice(s0, s0 + CONV_SUBTILE)
    c = c_scr[rows, :]
    mu = jnp.mean(c, axis=-1, keepdims=True)
    var = jnp.mean(jnp.square(c - mu), axis=-1, keepdims=True)
    y = (c - mu) * lax.rsqrt(var + EPS) * gcl_ref[...] + bcl_ref[...]
    pw = jnp.dot(_silu(y).astype(BF16), wpw_ref[...], preferred_element_type=F32)
    return (pw * sc_ref[rows, :].astype(F32)).astype(BF16)


def _decode_conv_kernel(hist_ref, wdw_ref, bdw_ref, gcl_ref, bcl_ref, wpw_ref, sc_ref, o_ref):
    kw = hist_ref.shape[0]
    acc = jnp.broadcast_to(bdw_ref[...], o_ref.shape)
    for j in range(kw):
        acc = acc + wdw_ref[j:j + 1, :] * hist_ref[j]
    mu = jnp.mean(acc, axis=-1, keepdims=True)
    var = jnp.mean(jnp.square(acc - mu), axis=-1, keepdims=True)
    y = (acc - mu) * lax.rsqrt(var + EPS) * gcl_ref[...] + bcl_ref[...]
    c = jnp.dot(_silu(y).astype(BF16), wpw_ref[...], preferred_element_type=F32)
    o_ref[...] = (c * sc_ref[...].astype(F32)).astype(o_ref.dtype)


def _decode_conv(hist_t, sc, w_dw, b_dw, g_cln, b_cln, w_pw_bf):
    kw, db, cw = hist_t.shape
    vec = lambda a: a.reshape(1, cw)
    return pl.pallas_call(
        _decode_conv_kernel,
        out_shape=jax.ShapeDtypeStruct((db, cw), BF16),
        compiler_params=pltpu.CompilerParams(vmem_limit_bytes=VMEM_LIMIT),
        name="decode_conv",
    )(hist_t, w_dw, vec(b_dw), vec(g_cln), vec(b_cln), w_pw_bf, sc)


def _out_rows(x, oa, oc, wout_ref):
    aw = oa.shape[1]
    h = x + jnp.dot(oa, wout_ref[0:aw, :], preferred_element_type=F32)
    return h + jnp.dot(oc, wout_ref[aw:, :], preferred_element_type=F32)


def _ple_rows(rows, h, p_ref, gple_ref, wpg_ref, wple_ref, gfin_ref, y_ref, final_norm):
    r = _rms(h, gple_ref[...]).astype(BF16)
    gate = _sigmoid(jnp.dot(r, wpg_ref[...], preferred_element_type=F32))
    e = jnp.dot(p_ref[rows, :].astype(BF16), wple_ref[...], preferred_element_type=F32)
    h = h + gate * e
    y_ref[rows, :] = _rms(h, gfin_ref[...]) if final_norm else h


def _out_kernel(x_ref, oa_ref, oc_ref, p_ref, wout_ref, gple_ref, wpg_ref, wple_ref, gfin_ref, y_ref,
                *, final_norm):
    tm = oa_ref.shape[0]
    sub = min(tm, OUT_SUBTILE)
    for r0 in range(0, tm, sub):
        rows = slice(r0, r0 + sub)
        h = _out_rows(x_ref[rows, :], oa_ref[rows, :], oc_ref[rows, :], wout_ref)
        _ple_rows(rows, h, p_ref, gple_ref, wpg_ref, wple_ref, gfin_ref, y_ref, final_norm)


def _out_conv_kernel(x_ref, oa_ref, p_ref, wout_ref, gple_ref, wpg_ref, wple_ref, gfin_ref,
                     g0_ref, sc0_ref, gc_ref, gh_ref, sc_ref, wdw_ref, bdw_ref, gcl_ref, bcl_ref, wpw_ref,
                     y_ref, sh_scr, c_scr, oc_scr, *, final_norm, n_steps, tiles_per_seq):
    tm = oa_ref.shape[0]
    step = pl.program_id(0)
    conv_weights = (wdw_ref, bdw_ref, gcl_ref, bcl_ref, wpw_ref)
    sub_tiles = list(range(0, tm, CONV_SUBTILE))

    @pl.when(step == 0)
    def _first_conv():
        _conv_window(g0_ref, gh_ref, sh_scr, True)
        for s0 in sub_tiles:
            oc_scr[0, s0:s0 + CONV_SUBTILE, :] = _conv_rows(s0, *conv_weights, sc0_ref, sh_scr, c_scr)

    nxt = jnp.minimum(step + 1, n_steps - 1)
    h = _out_rows(x_ref[...], oa_ref[...], oc_scr[step % 2], wout_ref)
    _conv_window(gc_ref, gh_ref, sh_scr, nxt % tiles_per_seq == 0)
    for s0 in sub_tiles:
        oc_scr[(step + 1) % 2, s0:s0 + CONV_SUBTILE, :] = _conv_rows(s0, *conv_weights, sc_ref, sh_scr, c_scr)
    _ple_rows(slice(0, tm), h, p_ref, gple_ref, wpg_ref, wple_ref, gfin_ref, y_ref, final_norm)


def _outproj(x2d, oa, oc, p2d, w_out_bf, g_ple, w_pg_bf, w_ple_bf, g_final, tm, final_norm):
    m, d = x2d.shape
    aw, cw, pd = oa.shape[1], oc.shape[1], p2d.shape[1]
    row = lambda i: (i, 0)
    full = lambda i: (0, 0)
    once = pl.Buffered(1)
    return pl.pallas_call(
        functools.partial(_out_kernel, final_norm=final_norm),
        grid=(m // tm,),
        in_specs=[
            pl.BlockSpec((tm, d), row),
            pl.BlockSpec((tm, aw), row),
            pl.BlockSpec((tm, cw), row),
            pl.BlockSpec((tm, pd), row),
            pl.BlockSpec((aw + cw, d), full, pipeline_mode=once),
            pl.BlockSpec((1, d), full),
            pl.BlockSpec((d, d), full, pipeline_mode=once),
            pl.BlockSpec((pd, d), full, pipeline_mode=once),
            pl.BlockSpec((1, d), full),
        ],
        out_specs=pl.BlockSpec((tm, d), row),
        out_shape=jax.ShapeDtypeStruct((m, d), F32),
        compiler_params=pltpu.CompilerParams(
            dimension_semantics=("parallel",), vmem_limit_bytes=VMEM_LIMIT),
        name="outproj",
    )(x2d, oa, oc, p2d, w_out_bf, g_ple.reshape(1, d), w_pg_bf, w_ple_bf, g_final.reshape(1, d))


def _prompt_outproj(x2d, oa, g2d, sc2d, p2d, conv_params, w_out_bf, g_ple, w_pg_bf, w_ple_bf, g_final,
                    tm, seq_len, final_norm):
    m, d = x2d.shape
    aw, cw, pd = oa.shape[1], g2d.shape[1], p2d.shape[1]
    w_dw, b_dw, g_cln, b_cln, w_pw_bf = conv_params
    kw = w_dw.shape[0]
    halo = -(-(kw - 1) // SUBLANES) * SUBLANES
    assert tm % halo == 0 and seq_len % tm == 0 and tm % CONV_SUBTILE == 0
    hb = tm // halo
    n_steps = m // tm
    row = lambda i: (i, 0)
    full = lambda i: (0, 0)
    ahead = lambda i: (jnp.minimum(i + 1, n_steps - 1), 0)
    once = pl.Buffered(1)
    vec = lambda a, n: a.reshape(1, n)
    return pl.pallas_call(
        functools.partial(_out_conv_kernel, final_norm=final_norm, n_steps=n_steps, tiles_per_seq=seq_len // tm),
        grid=(n_steps,),
        in_specs=[
            pl.BlockSpec((tm, d), row),
            pl.BlockSpec((tm, aw), row),
            pl.BlockSpec((tm, pd), row),
            pl.BlockSpec((aw + cw, d), full, pipeline_mode=once),
            pl.BlockSpec((1, d), full),
            pl.BlockSpec((d, d), full, pipeline_mode=once),
            pl.BlockSpec((pd, d), full, pipeline_mode=once),
            pl.BlockSpec((1, d), full),
            pl.BlockSpec((tm, cw), full),
            pl.BlockSpec((tm, cw), full),
            pl.BlockSpec((tm, cw), ahead),
            pl.BlockSpec((halo, cw), lambda i: (jnp.maximum(ahead(i)[0] * hb - 1, 0), 0)),
            pl.BlockSpec((tm, cw), ahead),
            pl.BlockSpec((kw, cw), full),
            pl.BlockSpec((1, cw), full),
            pl.BlockSpec((1, cw), full),
            pl.BlockSpec((1, cw), full),
            pl.BlockSpec((cw, cw), full, pipeline_mode=once),
        ],
        out_specs=pl.BlockSpec((tm, d), row),
        out_shape=jax.ShapeDtypeStruct((m, d), F32),
        scratch_shapes=[pltpu.VMEM((SUBLANES, halo + tm, cw), F32), pltpu.VMEM((tm, cw), F32),
                        pltpu.VMEM((2, tm, cw), BF16)],
        compiler_params=pltpu.CompilerParams(
            dimension_semantics=("arbitrary",), vmem_limit_bytes=VMEM_LIMIT),
        name="prompt_outproj",
    )(x2d, oa, p2d, w_out_bf, vec(g_ple, d), w_pg_bf, w_ple_bf, vec(g_final, d),
      g2d, sc2d, g2d, g2d, sc2d, w_dw, vec(b_dw, cw), vec(g_cln, cw), vec(b_cln, cw), w_pw_bf)


def kernel(x_prompt, x_sample, p_prompt, p_sample, cache_k, cache_v, state_conv, page_table, w_in, g_norm, lam_q1, lam_k1, lam_q2, lam_k2, g_subln, w_dw, b_dw, g_cln, b_cln, w_pw, w_out, g_ple, w_pg, w_ple, g_final):
    b, s, d = x_prompt.shape
    db, ds, _ = x_sample.shape
    depth = w_in.shape[0]
    nh, hw = cache_k.shape[3], cache_k.shape[4]
    aw = nh * hw
    cw = w_dw.shape[2]
    kw = w_dw.shape[1]
    past = page_table.shape[1] * cache_k.shape[2]
    assert ds == 1 and aw == cw and hw == 2 * HEAD_DIM

    tab_p = _rope_tables(jnp.arange(s, dtype=jnp.int32))
    tab_s = _rope_tables(jnp.full((db * ds,), past, dtype=jnp.int32))

    h_p = x_prompt.reshape(b * s, d)
    h_s = x_sample.reshape(db * ds, d)
    outs = {k: [] for k in ("kp", "vp", "cp", "ks", "vs", "cs")}
    for i in range(depth):
        lam_init = _lam_init(i)
        last = i == depth - 1
        lams = tuple(a[i].reshape(1, HEAD_DIM) for a in (lam_q1, lam_k1, lam_q2, lam_k2))
        w_in_bf, w_pw_bf = w_in[i].astype(BF16), w_pw[i].astype(BF16)
        w_out_bf, w_pg_bf, w_ple_bf = w_out[i].astype(BF16), w_pg[i].astype(BF16), w_ple[i].astype(BF16)

        q, k, v, qd, kd, vd = _inproj(h_p, h_s, g_norm[i], w_in_bf, tab_p, tab_s, tm=1024, seg=aw,
                                      kinds=("q", "k", "v"))
        sa, g, sc, sad, gd, scd = _inproj(h_p, h_s, g_norm[i], w_in_bf, tab_p, tab_s, tm=1024, seg=aw,
                                          kinds=("gate_a", "glu_a", "glu_b", "gate_c"))
        as3 = lambda a: a.reshape(b, s, -1)
        dec_ops = tuple(a.reshape(db, nh, hw) for a in (qd, kd, vd, sad))
        caches = (cache_k, cache_v)

        oa, oad = _attention(as3(q), as3(k), as3(v), as3(sa), dec_ops, caches, page_table, i, 0, db,
                             lams, g_subln[i], tq=256, lam_init=lam_init)
        h_p = _prompt_outproj(h_p, oa.reshape(b * s, aw), g, sc, p_prompt[i].reshape(b * s, -1),
                              (w_dw[i], b_dw[i], g_cln[i], b_cln[i], w_pw_bf), w_out_bf, g_ple[i], w_pg_bf,
                              w_ple_bf, g_final, tm=256, seq_len=s, final_norm=last)
        outs["kp"].append(k.reshape(b, s, nh, hw))
        outs["vp"].append(v.reshape(b, s, nh, hw))
        outs["cp"].append(as3(g)[:, s - (kw - 1):])

        hist = jnp.concatenate([state_conv[i], gd.reshape(db, ds, cw)], axis=1)
        ocd = _decode_conv(hist.transpose(1, 0, 2), scd, w_dw[i], b_dw[i], g_cln[i], b_cln[i], w_pw_bf)
        h_s = _outproj(h_s, oad.reshape(db, aw).astype(BF16), ocd, p_sample[i].reshape(db * ds, -1),
                       w_out_bf, g_ple[i], w_pg_bf, w_ple_bf, g_final, tm=db * ds, final_norm=last)
        outs["ks"].append(kd.reshape(db, ds, nh, hw))
        outs["vs"].append(vd.reshape(db, ds, nh, hw))
        outs["cs"].append(hist[:, 1:])

    st = lambda key: jnp.stack(outs[key])
    return (h_p.reshape(b, s, d), h_s.reshape(db, ds, d),
            st("kp"), st("vp"), st("cp"), st("ks"), st("vs"), st("cs"))
```
